```python
import jax, jax.numpy as jnp
from jax import lax
import numpy as np

D_MODEL = 1024
BATCH = 8
SEQ = 4096
DEPTH = 4

GRID_W = 64
N_FOURIER_GROUPS = 4
FOURIER_GROUP_DIM = D_MODEL // N_FOURIER_GROUPS
N_HEADS = 16
HEAD_DIM = D_MODEL // N_HEADS
WIN_H_MAX = 8
WIN_W = 16
D_FF_DENSE = 2816
N_EXPERTS = 8
TOP_K = 2
D_FF_EXPERT = 3584
MOE_BLOCK = 256
RMS_EPS = 1e-6
N_EVEN = (DEPTH + 1) // 2
N_ODD = DEPTH // 2

kernel_name = "hybrid_fnet_natten_moe_encoder"


def rms_norm(x, g):
    xf = x.astype(jnp.float32)
    y = xf * lax.rsqrt(jnp.mean(xf * xf, axis=-1, keepdims=True) + RMS_EPS)
    return (y * g.astype(jnp.float32)).astype(x.dtype)


def fourier_mixer(h, w_in, w_out):
    B, S, D = h.shape
    u = (h @ w_in).reshape(B, S, N_FOURIER_GROUPS, FOURIER_GROUP_DIM)
    f = jnp.fft.fft2(u.astype(jnp.float32), axes=(1, 3), norm="ortho").real
    return f.astype(h.dtype).reshape(B, S, D) @ w_out


def neighbourhood_attention(h, w_qkv, rpb, w_o):
    B, S, D = h.shape
    rows = S // GRID_W
    kh = min(WIN_H_MAX, rows)
    qkv = (h @ w_qkv).reshape(B, rows, GRID_W, 3, N_HEADS, HEAD_DIM)
    q = qkv[:, :, :, 0] * (HEAD_DIM ** -0.5)
    k = qkv[:, :, :, 1]
    v = qkv[:, :, :, 2]
    r = jnp.arange(rows)
    row_start = jnp.clip(r - kh // 2, 0, rows - kh)
    c = jnp.arange(GRID_W)
    col_start = jnp.clip(c - WIN_W // 2, 0, GRID_W - WIN_W)
    col_idx = col_start[:, None] + jnp.arange(WIN_W)[None, :]
    dr = row_start[:, None] + jnp.arange(kh)[None, :] - r[:, None] + (WIN_H_MAX - 1)
    dc = col_idx - c[:, None] + (WIN_W - 1)
    bias = rpb[:, dr[:, None, :, None], dc[None, :, None, :]]
    bias = jnp.moveaxis(bias, 1, 0).astype(jnp.float32)

    def row_block(args):
        q_r, s, b_r = args
        k_rows = lax.dynamic_slice_in_dim(k, s, kh, axis=1)
        v_rows = lax.dynamic_slice_in_dim(v, s, kh, axis=1)
        k_g = k_rows[:, :, col_idx]
        v_g = v_rows[:, :, col_idx]
        sc = jnp.einsum('bwhd,bawkhd->bhwak', q_r, k_g).astype(jnp.float32) + b_r[None]
        p = jax.nn.softmax(sc.reshape(B, N_HEADS, GRID_W, kh * WIN_W), axis=-1)
        p = p.reshape(sc.shape).astype(v.dtype)
        return jnp.einsum('bhwak,bawkhd->bwhd', p, v_g)

    out = lax.map(row_block, (jnp.moveaxis(q, 1, 0), row_start, bias))
    out = jnp.moveaxis(out, 0, 1).reshape(B, S, D)
    return out @ w_o


def swiglu(h, w_gate, w_up, w_down):
    return (jax.nn.silu(h @ w_gate) * (h @ w_up)) @ w_down


def moe_swiglu(h, w_router, b_router, w_gate, w_up, w_down):
    B, S, D = h.shape
    t = h.reshape(-1, D)
    T = t.shape[0]
    TK = T * TOP_K
    logits = (t @ w_router).astype(jnp.float32) + b_router.astype(jnp.float32)
    top_val, top_idx = lax.top_k(logits, TOP_K)
    gates = jax.nn.softmax(top_val, axis=-1)
    flat_e = top_idx.reshape(-1)
    flat_tok = jnp.arange(TK, dtype=jnp.int32) // TOP_K
    order = jnp.argsort(flat_e)
    e_sorted = flat_e[order]
    tok_sorted = flat_tok[order]
    g_sorted = gates.reshape(-1)[order]
    counts = jnp.bincount(flat_e, length=N_EXPERTS)
    padded = (counts + MOE_BLOCK - 1) // MOE_BLOCK * MOE_BLOCK
    start = jnp.cumsum(counts) - counts
    pend = jnp.cumsum(padded)
    pstart = pend - padded
    dest = pstart[e_sorted] + jnp.arange(TK) - start[e_sorted]
    n_blocks = -(-(TK + N_EXPERTS * (MOE_BLOCK - 1)) // MOE_BLOCK)
    row_tok = jnp.zeros((n_blocks * MOE_BLOCK,), jnp.int32).at[dest].set(tok_sorted)
    block_expert = jnp.minimum(
        jnp.searchsorted(pend, jnp.arange(n_blocks) * MOE_BLOCK, side='right'), N_EXPERTS - 1)
    xs = t[row_tok].reshape(n_blocks, MOE_BLOCK, D)

    def expert_block(args):
        xb, e = args
        return (jax.nn.silu(xb @ w_gate[e]) * (xb @ w_up[e])) @ w_down[e]

    ys = lax.map(expert_block, (xs, block_expert)).reshape(-1, D)
    y = ys[dest] * g_sorted[:, None].astype(h.dtype)
    out = jnp.zeros((T, D), h.dtype).at[tok_sorted].add(y)
    return out.reshape(B, S, D)


def setup_inputs(seed: int = 0) -> dict:
    key = jax.random.key(seed)
    ks = jax.random.split(key, 20)

    def w(k, shape, fan_in):
        return jax.random.normal(k, shape, jnp.float32) * (fan_in ** -0.5)

    return {
        "x": jax.random.normal(ks[0], (BATCH, SEQ, D_MODEL), jnp.float32),
        "norm_mix": 1.0 + 0.02 * jax.random.normal(ks[1], (DEPTH, D_MODEL), jnp.float32),
        "norm_ffn": 1.0 + 0.02 * jax.random.normal(ks[2], (DEPTH, D_MODEL), jnp.float32),
        "fourier_w_in": w(ks[3], (N_EVEN, D_MODEL, D_MODEL), D_MODEL),
        "fourier_w_out": w(ks[4], (N_EVEN, D_MODEL, D_MODEL), D_MODEL),
        "nat_w_qkv": w(ks[5], (N_ODD, D_MODEL, 3 * D_MODEL), D_MODEL),
        "nat_rpb": 0.1 * jax.random.normal(ks[6], (N_ODD, N_HEADS, 2 * WIN_H_MAX - 1, 2 * WIN_W - 1), jnp.float32),
        "nat_w_o": w(ks[7], (N_ODD, D_MODEL, D_MODEL), D_MODEL),
        "ffn_w_gate": w(ks[8], (N_EVEN, D_MODEL, D_FF_DENSE), D_MODEL),
        "ffn_w_up": w(ks[9], (N_EVEN, D_MODEL, D_FF_DENSE), D_MODEL),
        "ffn_w_down": w(ks[10], (N_EVEN, D_FF_DENSE, D_MODEL), D_FF_DENSE),
        "router_w": w(ks[11], (N_ODD, D_MODEL, N_EXPERTS), D_MODEL),
        "router_b": 0.01 * jax.random.normal(ks[12], (N_ODD, N_EXPERTS), jnp.float32),
        "expert_w_gate": w(ks[13], (N_ODD, N_EXPERTS, D_MODEL, D_FF_EXPERT), D_MODEL),
        "expert_w_up": w(ks[14], (N_ODD, N_EXPERTS, D_MODEL, D_FF_EXPERT), D_MODEL),
        "expert_w_down": w(ks[15], (N_ODD, N_EXPERTS, D_FF_EXPERT, D_MODEL), D_FF_EXPERT),
        "final_norm": 1.0 + 0.02 * jax.random.normal(ks[16], (D_MODEL,), jnp.float32),
    }


def reference(x, norm_mix, norm_ffn, fourier_w_in, fourier_w_out, nat_w_qkv, nat_rpb, nat_w_o,
              ffn_w_gate, ffn_w_up, ffn_w_down, router_w, router_b,
              expert_w_gate, expert_w_up, expert_w_down, final_norm):
    h = x
    for i in range(DEPTH):
        j = i // 2
        hn = rms_norm(h, norm_mix[i])
        if i % 2 == 0:
            mix = fourier_mixer(hn, fourier_w_in[j], fourier_w_out[j])
        else:
            mix = neighbourhood_attention(hn, nat_w_qkv[j], nat_rpb[j], nat_w_o[j])
        h = h + mix
        hn = rms_norm(h, norm_ffn[i])
        if i % 2 == 0:
            ff = swiglu(hn, ffn_w_gate[j], ffn_w_up[j], ffn_w_down[j])
        else:
            ff = moe_swiglu(hn, router_w[j], router_b[j], expert_w_gate[j], expert_w_up[j], expert_w_down[j])
        h = h + ff
    return rms_norm(h, final_norm)
```

```python
import functools

import numpy as np
import jax
import jax.numpy as jnp
from jax import lax
from jax.experimental import pallas as pl
from jax.experimental.pallas import tpu as pltpu

F32 = jnp.float32
BF16 = jnp.bfloat16
I32 = jnp.int32

GRID_W = 64
N_FOURIER_GROUPS = 4
TOP_K = 2
RMS_EPS = 1e-6
MASK_VALUE = -1e30

V7X_VMEM_BYTES = 64 * 1024 * 1024
VMEM_LIMIT_BYTES = V7X_VMEM_BYTES - 8 * 1024 * 1024


def _tile(n, target):
    if n <= target:
        return n
    for t in range(target, 0, -1):
        if n % t == 0:
            return t
    return n


def _params(*sem):
    return pltpu.CompilerParams(dimension_semantics=sem, vmem_limit_bytes=VMEM_LIMIT_BYTES)


def _rms(x, g):
    ms = jnp.mean(x * x, axis=-1, keepdims=True)
    return x * lax.rsqrt(ms + RMS_EPS) * g


def _dot(a, b):
    return jnp.dot(a, b, preferred_element_type=F32)


def _norm_matmul_kernel(x_ref, g_ref, w_ref, o_ref, xn_ref):
    @pl.when(pl.program_id(1) == 0)
    def _():
        xn_ref[...] = _rms(x_ref[...], g_ref[...]).astype(BF16)

    o_ref[...] = _dot(xn_ref[...], w_ref[...]).astype(o_ref.dtype)


def norm_matmul(h, g, w, *, tm=1024, tn=1024):
    M, K = h.shape
    N = w.shape[1]
    tm, tn = _tile(M, tm), _tile(N, tn)
    return pl.pallas_call(
        _norm_matmul_kernel,
        grid=(M // tm, N // tn),
        in_specs=[
            pl.BlockSpec((tm, K), lambda i, j: (i, 0)),
            pl.BlockSpec((1, K), lambda i, j: (0, 0)),
            pl.BlockSpec((K, tn), lambda i, j: (0, j)),
        ],
        out_specs=pl.BlockSpec((tm, tn), lambda i, j: (i, j)),
        out_shape=jax.ShapeDtypeStruct((M, N), BF16),
        scratch_shapes=[pltpu.VMEM((tm, K), BF16)],
        compiler_params=_params("parallel", "arbitrary"),
        name="norm_matmul",
    )(h, g.reshape(1, K), w)


def _matmul_res_kernel(a_ref, w_ref, r_ref, o_ref):
    o_ref[...] = r_ref[...] + _dot(a_ref[...], w_ref[...])


def matmul_res(a, w, res, *, tm=1024, tn=1024):
    M, K = a.shape
    N = w.shape[1]
    tm, tn = _tile(M, tm), _tile(N, tn)
    return pl.pallas_call(
        _matmul_res_kernel,
        grid=(M // tm, N // tn),
        in_specs=[
            pl.BlockSpec((tm, K), lambda i, j: (i, 0)),
            pl.BlockSpec((K, tn), lambda i, j: (0, j)),
            pl.BlockSpec((tm, tn), lambda i, j: (i, j)),
        ],
        out_specs=pl.BlockSpec((tm, tn), lambda i, j: (i, j)),
        out_shape=jax.ShapeDtypeStruct((M, N), F32),
        compiler_params=_params("parallel", "parallel"),
        name="matmul_res",
    )(a, w, res)


def _matmul_f32_kernel(a_ref, b_ref, o_ref):
    o_ref[...] = jnp.dot(a_ref[...], b_ref[...], preferred_element_type=F32,
                         precision=lax.Precision.HIGHEST).astype(o_ref.dtype)


def matmul_f32(a, b, *, out_dtype, tm=512, tn=512):
    M, K = a.shape
    N = b.shape[1]
    tm, tn = _tile(M, tm), _tile(N, tn)
    return pl.pallas_call(
        _matmul_f32_kernel,
        grid=(M // tm, N // tn),
        in_specs=[
            pl.BlockSpec((tm, K), lambda i, j: (i, 0)),
            pl.BlockSpec((K, tn), lambda i, j: (0, j)),
        ],
        out_specs=pl.BlockSpec((tm, tn), lambda i, j: (i, j)),
        out_shape=jax.ShapeDtypeStruct((M, N), out_dtype),
        compiler_params=_params("parallel", "parallel"),
        name="matmul_f32",
    )(a, b)


def _dft_out_kernel(d_ref, ab_ref, w_ref, r_ref, o_ref, acc_ref, *, scale):
    k = pl.program_id(2)

    @pl.when(k == 0)
    def _():
        acc_ref[...] = jnp.zeros_like(acc_ref)

    acc_ref[...] += _dot(d_ref[...], ab_ref[...])

    @pl.when(k == pl.num_programs(2) - 1)
    def _():
        f = (acc_ref[...] * scale).astype(BF16)
        o_ref[...] = r_ref[...] + _dot(f, w_ref[...])


def dft_out(dft, ab, w_out, res, *, batch, scale, tm=1024, tk=1024):
    S = dft.shape[0]
    D = w_out.shape[0]
    tm, tk = _tile(S, tm), _tile(S, tk)
    nks = S // tk
    nm = S // tm
    return pl.pallas_call(
        functools.partial(_dft_out_kernel, scale=scale),
        grid=(batch, nm, 2 * nks),
        in_specs=[
            pl.BlockSpec((tm, tk), lambda b, m, k: (m, k)),
            pl.BlockSpec((tk, D), lambda b, m, k: (b * nks + k % nks, k // nks)),
            pl.BlockSpec((D, D), lambda b, m, k: (0, 0)),
            pl.BlockSpec((tm, D), lambda b, m, k: (b * nm + m, 0)),
        ],
        out_specs=pl.BlockSpec((tm, D), lambda b, m, k: (b * nm + m, 0)),
        out_shape=jax.ShapeDtypeStruct(res.shape, F32),
        scratch_shapes=[pltpu.VMEM((tm, D), F32)],
        compiler_params=_params("parallel", "parallel", "arbitrary"),
        name="dft_out",
    )(dft, ab, w_out, res)


def _ffn_kernel(x_ref, g_ref, wg_ref, wu_ref, wd_ref, o_ref, xn_ref, acc_ref):
    j = pl.program_id(1)

    @pl.when(j == 0)
    def _():
        xn_ref[...] = _rms(x_ref[...], g_ref[...]).astype(BF16)
        acc_ref[...] = jnp.zeros_like(acc_ref)

    xn = xn_ref[...]
    a = (jax.nn.silu(_dot(xn, wg_ref[...])) * _dot(xn, wu_ref[...])).astype(BF16)
    acc_ref[...] += _dot(a, wd_ref[...])

    @pl.when(j == pl.num_programs(1) - 1)
    def _():
        o_ref[...] = x_ref[...] + acc_ref[...]


def ffn(h, g, wg, wu, wd, *, tm=512, tf=1408):
    M, D = h.shape
    F = wg.shape[1]
    tm, tf = _tile(M, tm), _tile(F, tf)
    return pl.pallas_call(
        _ffn_kernel,
        grid=(M // tm, F // tf),
        in_specs=[
            pl.BlockSpec((tm, D), lambda i, j: (i, 0)),
            pl.BlockSpec((1, D), lambda i, j: (0, 0)),
            pl.BlockSpec((D, tf), lambda i, j: (0, j)),
            pl.BlockSpec((D, tf), lambda i, j: (0, j)),
            pl.BlockSpec((tf, D), lambda i, j: (j, 0)),
        ],
        out_specs=pl.BlockSpec((tm, D), lambda i, j: (i, 0)),
        out_shape=jax.ShapeDtypeStruct((M, D), F32),
        scratch_shapes=[pltpu.VMEM((tm, D), BF16), pltpu.VMEM((tm, D), F32)],
        compiler_params=_params("parallel", "arbitrary"),
        name="ffn",
    )(h, g.reshape(1, D), wg, wu, wd)


def _attn_kernel(q_ref, k_ref, v_ref, b_ref, o_ref, *, n_heads, scale):
    W, D = q_ref.shape
    hd = D // n_heads
    nk = k_ref.shape[1] * k_ref.shape[2]
    for h in range(n_heads):
        sl = slice(h * hd, (h + 1) * hd)
        q = q_ref[:, sl]
        k = k_ref[0, :, :, sl].reshape(nk, hd)
        v = v_ref[0, :, :, sl].reshape(nk, hd)
        s = lax.dot_general(q, k, (((1,), (1,)), ((), ())), preferred_element_type=F32)
        s = s * scale + b_ref[h]
        p = jnp.exp(s - jnp.max(s, axis=-1, keepdims=True))
        l = jnp.sum(p, axis=-1, keepdims=True)
        o_ref[:, sl] = (_dot(p.astype(BF16), v) / l).astype(o_ref.dtype)


def _attn_bias_table(rpb, rows, kh, kw):
    H = rpb.shape[0]
    win_h_max = (rpb.shape[1] + 1) // 2
    c = np.arange(GRID_W)
    col_start = np.clip(c - kw // 2, 0, GRID_W - kw)
    kc = np.arange(GRID_W)
    valid = (kc[None, :] >= col_start[:, None]) & (kc[None, :] < col_start[:, None] + kw)
    dc = np.clip(kc[None, :] - c[:, None] + (kw - 1), 0, 2 * kw - 2)
    t = np.arange(kh)
    a = np.arange(kh)
    dr = np.clip(a[None, :] - t[:, None] + (win_h_max - 1), 0, 2 * win_h_max - 2)
    tab = rpb[:, dr[:, :, None, None], dc[None, None, :, :]]
    tab = jnp.where(jnp.asarray(valid)[None, None, None], tab.astype(F32), MASK_VALUE)
    tab = jnp.transpose(tab, (1, 0, 3, 2, 4))
    return tab.reshape(kh, H, GRID_W, kh * GRID_W)


def attention(qkv, rpb, *, batch):
    T, D3 = qkv.shape
    D = D3 // 3
    S = T // batch
    rows = S // GRID_W
    H = rpb.shape[0]
    kh = min((rpb.shape[1] + 1) // 2, rows)
    kw = (rpb.shape[2] + 1) // 2
    hd = D // H
    bias = _attn_bias_table(rpb, rows, kh, kw)
    qkv4 = qkv.reshape(batch, rows, GRID_W, D3)

    def row_start(r):
        return jnp.clip(r - kh // 2, 0, rows - kh)

    def kv_spec(which):
        return pl.BlockSpec(
            (pl.Element(1), pl.Element(kh), pl.Element(GRID_W), pl.Element(D)),
            lambda r, b: (b, row_start(r), 0, which * D))

    out = pl.pallas_call(
        functools.partial(_attn_kernel, n_heads=H, scale=hd ** -0.5),
        grid=(rows, batch),
        in_specs=[
            pl.BlockSpec((None, None, GRID_W, D), lambda r, b: (b, r, 0, 0)),
            kv_spec(1),
            kv_spec(2),
            pl.BlockSpec((None, H, GRID_W, kh * GRID_W), lambda r, b: (r - row_start(r), 0, 0, 0)),
        ],
        out_specs=pl.BlockSpec((None, None, GRID_W, D), lambda r, b: (b, r, 0, 0)),
        out_shape=jax.ShapeDtypeStruct((batch, rows, GRID_W, D), BF16),
        compiler_params=_params("parallel", "parallel"),
        name="attention",
    )(qkv4, qkv4, qkv4, bias)
    return out.reshape(T, D)


def _router_kernel(x_ref, g_ref, wr_ref, br_ref, hn_ref, idx_ref, gate_ref):
    hn = _rms(x_ref[...], g_ref[...])
    hn_ref[...] = hn
    logits = jnp.dot(hn, wr_ref[...], preferred_element_type=F32,
                     precision=lax.Precision.HIGHEST) + br_ref[...]
    n_e = logits.shape[1]
    lane = lax.broadcasted_iota(I32, logits.shape, 1)
    v1 = jnp.max(logits, axis=-1, keepdims=True)
    i1 = jnp.min(jnp.where(logits == v1, lane, n_e), axis=-1, keepdims=True)
    rest = jnp.where(lane == i1, -jnp.inf, logits)
    v2 = jnp.max(rest, axis=-1, keepdims=True)
    i2 = jnp.min(jnp.where(rest == v2, lane, n_e), axis=-1, keepdims=True)
    e2 = jnp.exp(v2 - v1)
    first = lax.broadcasted_iota(I32, idx_ref.shape, 1) == 0
    idx_ref[...] = jnp.where(first, i1, i2)
    gate_ref[...] = jnp.where(first, 1.0, e2) / (1.0 + e2)


def router(h, g, wr, br, *, tm=1024):
    M, D = h.shape
    E = wr.shape[1]
    tm = _tile(M, tm)
    return pl.pallas_call(
        _router_kernel,
        grid=(M // tm,),
        in_specs=[
            pl.BlockSpec((tm, D), lambda i: (i, 0)),
            pl.BlockSpec((1, D), lambda i: (0, 0)),
            pl.BlockSpec((D, E), lambda i: (0, 0)),
            pl.BlockSpec((1, E), lambda i: (0, 0)),
        ],
        out_specs=[
            pl.BlockSpec((tm, D), lambda i: (i, 0)),
            pl.BlockSpec((tm, TOP_K), lambda i: (i, 0)),
            pl.BlockSpec((tm, TOP_K), lambda i: (i, 0)),
        ],
        out_shape=[
            jax.ShapeDtypeStruct((M, D), F32),
            jax.ShapeDtypeStruct((M, TOP_K), I32),
            jax.ShapeDtypeStruct((M, TOP_K), F32),
        ],
        compiler_params=_params("parallel"),
        name="router",
    )(h, g.reshape(1, D), wr, br.reshape(1, E))


def _row_copy(src_hbm, dst_ref, sem, src_row, dst_row):
    return pltpu.make_async_copy(src_hbm.at[pl.ds(src_row, 1), :], dst_ref.at[pl.ds(dst_row, 1), :], sem)


def _gather_kernel(nu_ref, idx_ref, src_hbm, o_ref, buf_ref, sem):
    tm = buf_ref.shape[0]
    used = pl.program_id(0) < nu_ref[0]

    @pl.when(used)
    def _():
        def issue(r, carry):
            _row_copy(src_hbm, buf_ref, sem, idx_ref[r], r).start()
            return carry

        lax.fori_loop(0, tm, issue, 0)
        pltpu.make_async_copy(src_hbm.at[pl.ds(0, tm), :], buf_ref, sem).wait()
        o_ref[...] = buf_ref[...].astype(o_ref.dtype)

    @pl.when(jnp.logical_not(used))
    def _():
        o_ref[...] = jnp.zeros_like(o_ref)


def gather_rows(src, idx, n_used, *, tm):
    R = idx.shape[0]
    D = src.shape[1]
    return pl.pallas_call(
        _gather_kernel,
        grid_spec=pltpu.PrefetchScalarGridSpec(
            num_scalar_prefetch=1,
            grid=(R // tm,),
            in_specs=[
                pl.BlockSpec((tm,), lambda i, nu: (i,), memory_space=pltpu.SMEM),
                pl.BlockSpec(memory_space=pl.ANY),
            ],
            out_specs=pl.BlockSpec((tm, D), lambda i, nu: (i, 0)),
            scratch_shapes=[pltpu.VMEM((tm, D), src.dtype), pltpu.SemaphoreType.DMA],
        ),
        out_shape=jax.ShapeDtypeStruct((R, D), BF16),
        compiler_params=_params("arbitrary"),
        name="gather_rows",
    )(n_used, idx, src)


def _expert_kernel(be_ref, nu_ref, x_ref, gate_ref, wg_ref, wu_ref, wd_ref, o_ref, acc_ref):
    i, j = pl.program_id(0), pl.program_id(1)
    last = pl.num_programs(1) - 1
    used = i < nu_ref[0]

    @pl.when(used)
    def _():
        @pl.when(j == 0)
        def _():
            acc_ref[...] = jnp.zeros_like(acc_ref)

        x = x_ref[...]
        a = (jax.nn.silu(_dot(x, wg_ref[...])) * _dot(x, wu_ref[...])).astype(BF16)
        acc_ref[...] += _dot(a, wd_ref[...])

        @pl.when(j == last)
        def _():
            o_ref[...] = acc_ref[...] * gate_ref[...]

    @pl.when(jnp.logical_not(used) & (j == last))
    def _():
        o_ref[...] = jnp.zeros_like(o_ref)


def expert_ffn(xs, row_gate, block_expert, n_used, wg, wu, wd, *, tm, tf=1792):
    R, D = xs.shape
    F = wg.shape[2]
    tf = _tile(F, tf)
    nf = F // tf

    def fstep(i, j, nu):
        return jnp.where(i < nu[0], j, nf - 1)

    return pl.pallas_call(
        _expert_kernel,
        grid_spec=pltpu.PrefetchScalarGridSpec(
            num_scalar_prefetch=2,
            grid=(R // tm, nf),
            in_specs=[
                pl.BlockSpec((tm, D), lambda i, j, be, nu: (i, 0)),
                pl.BlockSpec((tm, 1), lambda i, j, be, nu: (i, 0)),
                pl.BlockSpec((None, D, tf), lambda i, j, be, nu: (be[i], 0, fstep(i, j, nu))),
                pl.BlockSpec((None, D, tf), lambda i, j, be, nu: (be[i], 0, fstep(i, j, nu))),
                pl.BlockSpec((None, tf, D), lambda i, j, be, nu: (be[i], fstep(i, j, nu), 0)),
            ],
            out_specs=pl.BlockSpec((tm, D), lambda i, j, be, nu: (i, 0)),
            scratch_shapes=[pltpu.VMEM((tm, D), F32)],
        ),
        out_shape=jax.ShapeDtypeStruct((R, D), F32),
        compiler_params=_params("parallel", "arbitrary"),
        name="expert_ffn",
    )(block_expert, n_used, xs, row_gate.reshape(R, 1), wg, wu, wd)


def _combine_kernel(dest_ref, ys_hbm, h_ref, g_ref, o_ref, buf_ref, sem, *, final_norm):
    tq = h_ref.shape[0]

    def issue(t, carry):
        for k in range(TOP_K):
            _row_copy(ys_hbm, buf_ref, sem, dest_ref[TOP_K * t + k], k * tq + t).start()
        return carry

    lax.fori_loop(0, tq, issue, 0)
    pltpu.make_async_copy(ys_hbm.at[pl.ds(0, TOP_K * tq), :], buf_ref, sem).wait()
    out = h_ref[...] + (buf_ref[pl.ds(0, tq), :] + buf_ref[pl.ds(tq, tq), :])
    if final_norm:
        out = _rms(out, g_ref[...])
    o_ref[...] = out


def combine(h, ys, dest, final_gain, *, tq=256):
    T, D = h.shape
    tq = _tile(T, tq)
    final_norm = final_gain is not None
    gain = final_gain if final_norm else jnp.ones((D,), F32)
    return pl.pallas_call(
        functools.partial(_combine_kernel, final_norm=final_norm),
        grid=(T // tq,),
        in_specs=[
            pl.BlockSpec((TOP_K * tq,), lambda i: (i,), memory_space=pltpu.SMEM),
            pl.BlockSpec(memory_space=pl.ANY),
            pl.BlockSpec((tq, D), lambda i: (i, 0)),
            pl.BlockSpec((1, D), lambda i: (0, 0)),
        ],
        out_specs=pl.BlockSpec((tq, D), lambda i: (i, 0)),
        out_shape=jax.ShapeDtypeStruct((T, D), F32),
        scratch_shapes=[pltpu.VMEM((TOP_K * tq, D), F32), pltpu.SemaphoreType.DMA],
        compiler_params=_params("arbitrary"),
        name="combine",
    )(dest, ys, h, gain.reshape(1, D))


def _final_norm_kernel(x_ref, g_ref, o_ref):
    o_ref[...] = _rms(x_ref[...], g_ref[...])


def final_norm(h, g, *, tm=1024):
    M, D = h.shape
    tm = _tile(M, tm)
    return pl.pallas_call(
        _final_norm_kernel,
        grid=(M // tm,),
        in_specs=[pl.BlockSpec((tm, D), lambda i: (i, 0)), pl.BlockSpec((1, D), lambda i: (0, 0))],
        out_specs=pl.BlockSpec((tm, D), lambda i: (i, 0)),
        out_shape=jax.ShapeDtypeStruct((M, D), F32),
        compiler_params=_params("parallel"),
        name="final_norm",
    )(h, g.reshape(1, D))


def _dft_cos_sin(n):
    j = lax.broadcasted_iota(I32, (n, n), 0)
    k = lax.broadcasted_iota(I32, (n, n), 1)
    ang = ((j * k) % n).astype(F32) * (2.0 * np.pi / n)
    return jnp.cos(ang), jnp.sin(ang)


def _route(idx, gates, n_experts, tm):
    T = idx.shape[0]
    TK = T * TOP_K
    flat_e = idx.reshape(TK)
    onehot = (flat_e[:, None] == jnp.arange(n_experts, dtype=I32)[None, :]).astype(I32)
    csum = jnp.cumsum(onehot, axis=0)
    rank = jnp.sum(csum * onehot, axis=1) - 1
    counts = csum[-1]
    padded = (counts + tm - 1) // tm * tm
    pend = jnp.cumsum(padded)
    pstart = pend - padded
    dest = (pstart[flat_e] + rank).astype(I32)
    n_blocks = -(-(TK + n_experts * (tm - 1)) // tm)
    R = n_blocks * tm
    row_tok = jnp.zeros((R,), I32).at[dest].set(jnp.arange(TK, dtype=I32) // TOP_K, unique_indices=True)
    row_gate = jnp.zeros((R,), F32).at[dest].set(gates.reshape(TK), unique_indices=True)
    block_expert = jnp.minimum(
        jnp.searchsorted(pend, jnp.arange(n_blocks, dtype=I32) * tm, side="right"), n_experts - 1).astype(I32)
    n_used = (pend[-1:] // tm).astype(I32)
    return row_tok, row_gate, dest, block_expert, n_used


def _moe_block_rows(T):
    return _tile(T * TOP_K, 512)


def kernel(x, norm_mix, norm_ffn, fourier_w_in, fourier_w_out, nat_w_qkv, nat_rpb, nat_w_o,
           ffn_w_gate, ffn_w_up, ffn_w_down, router_w, router_b,
           expert_w_gate, expert_w_up, expert_w_down, final_norm_g):
    B, S, D = x.shape
    T = B * S
    depth = norm_mix.shape[0]
    n_experts = router_w.shape[2]
    cg = D // N_FOURIER_GROUPS

    cc, sc = _dft_cos_sin(cg)
    eye = jnp.eye(N_FOURIER_GROUPS, dtype=F32)
    chan = jnp.concatenate([jnp.kron(eye, cc), jnp.kron(eye, sc)], axis=1)
    cs, ss = _dft_cos_sin(S)
    seq_dft = jnp.concatenate([cs, -ss], axis=1).astype(BF16)
    dft_scale = float(1.0 / np.sqrt(S * cg))

    h = x.reshape(T, D)
    for i in range(depth):
        j = i // 2
        last = i == depth - 1
        if i % 2 == 0:
            w_ab = matmul_f32(fourier_w_in[j], chan, out_dtype=BF16)
            ab = norm_matmul(h, norm_mix[i], w_ab)
            h = dft_out(seq_dft, ab, fourier_w_out[j].astype(BF16), h, batch=B, scale=dft_scale)
            h = ffn(h, norm_ffn[i], ffn_w_gate[j].astype(BF16), ffn_w_up[j].astype(BF16),
                    ffn_w_down[j].astype(BF16))
            if last:
                h = final_norm(h, final_norm_g)
        else:
            qkv = norm_matmul(h, norm_mix[i], nat_w_qkv[j].astype(BF16))
            att = attention(qkv, nat_rpb[j], batch=B)
            h = matmul_res(att, nat_w_o[j].astype(BF16), h)
            hn, idx, gates = router(h, norm_ffn[i], router_w[j], router_b[j])
            tm = _moe_block_rows(T)
            row_tok, row_gate, dest, block_expert, n_used = _route(idx, gates, n_experts, tm)
            xs = gather_rows(hn, row_tok, n_used, tm=tm)
            ys = expert_ffn(xs, row_gate, block_expert, n_used, expert_w_gate[j].astype(BF16),
                            expert_w_up[j].astype(BF16), expert_w_down[j].astype(BF16), tm=tm)
            h = combine(h, ys, dest, final_norm_g if last else None)
    return h.reshape(B, S, D)
```

```python
import functools

import numpy as np
import jax
import jax.numpy as jnp
from jax import lax
from jax.experimental import pallas as pl
from jax.experimental.pallas import tpu as pltpu

F32 = jnp.float32
BF16 = jnp.bfloat16
I32 = jnp.int32

GRID_W = 64
N_FOURIER_GROUPS = 4
TOP_K = 2
RMS_EPS = 1e-6
MASK_VALUE = -1e30

LANES = 128

V7X_VMEM_BYTES = 64 * 1024 * 1024
VMEM_LIMIT_BYTES = V7X_VMEM_BYTES - 8 * 1024 * 1024


def _tile(n, target):
    if n <= target:
        return n
    for t in range(target, 0, -1):
        if n % t == 0:
            return t
    return n


def _params(*sem):
    return pltpu.CompilerParams(dimension_semantics=sem, vmem_limit_bytes=VMEM_LIMIT_BYTES)


def _rms(x, g):
    ms = jnp.mean(x * x, axis=-1, keepdims=True)
    return x * lax.rsqrt(ms + RMS_EPS) * g


def _dot(a, b):
    return jnp.dot(a, b, preferred_element_type=F32)


def _norm_matmul_kernel(x_ref, g_ref, w_ref, o_ref, xn_ref):
    @pl.when(pl.program_id(1) == 0)
    def _():
        xn_ref[...] = _rms(x_ref[...], g_ref[...]).astype(BF16)

    o_ref[...] = _dot(xn_ref[...], w_ref[...]).astype(o_ref.dtype)


def norm_matmul(h, g, w, *, tm=1024, tn=1024):
    M, K = h.shape
    N = w.shape[1]
    tm, tn = _tile(M, tm), _tile(N, tn)
    return pl.pallas_call(
        _norm_matmul_kernel,
        grid=(M // tm, N // tn),
        in_specs=[
            pl.BlockSpec((tm, K), lambda i, j: (i, 0)),
            pl.BlockSpec((1, K), lambda i, j: (0, 0)),
            pl.BlockSpec((K, tn), lambda i, j: (0, j)),
        ],
        out_specs=pl.BlockSpec((tm, tn), lambda i, j: (i, j)),
        out_shape=jax.ShapeDtypeStruct((M, N), BF16),
        scratch_shapes=[pltpu.VMEM((tm, K), BF16)],
        compiler_params=_params("parallel", "arbitrary"),
        name="norm_matmul",
    )(h, g.reshape(1, K), w)


def _matmul_res_kernel(a_ref, w_ref, r_ref, o_ref):
    o_ref[...] = r_ref[...] + _dot(a_ref[...], w_ref[...])


def matmul_res(a, w, res, *, tm=1024, tn=1024):
    M, K = a.shape
    N = w.shape[1]
    tm, tn = _tile(M, tm), _tile(N, tn)
    return pl.pallas_call(
        _matmul_res_kernel,
        grid=(M // tm, N // tn),
        in_specs=[
            pl.BlockSpec((tm, K), lambda i, j: (i, 0)),
            pl.BlockSpec((K, tn), lambda i, j: (0, j)),
            pl.BlockSpec((tm, tn), lambda i, j: (i, j)),
        ],
        out_specs=pl.BlockSpec((tm, tn), lambda i, j: (i, j)),
        out_shape=jax.ShapeDtypeStruct((M, N), F32),
        compiler_params=_params("parallel", "parallel"),
        name="matmul_res",
    )(a, w, res)


def _matmul_f32_kernel(a_ref, b_ref, o_ref):
    o_ref[...] = jnp.dot(a_ref[...], b_ref[...], preferred_element_type=F32,
                         precision=lax.Precision.HIGHEST).astype(o_ref.dtype)


def matmul_f32(a, b, *, out_dtype, tm=512, tn=512):
    M, K = a.shape
    N = b.shape[1]
    tm, tn = _tile(M, tm), _tile(N, tn)
    return pl.pallas_call(
        _matmul_f32_kernel,
        grid=(M // tm, N // tn),
        in_specs=[
            pl.BlockSpec((tm, K), lambda i, j: (i, 0)),
            pl.BlockSpec((K, tn), lambda i, j: (0, j)),
        ],
        out_specs=pl.BlockSpec((tm, tn), lambda i, j: (i, j)),
        out_shape=jax.ShapeDtypeStruct((M, N), out_dtype),
        compiler_params=_params("parallel", "parallel"),
        name="matmul_f32",
    )(a, b)


def _dft_out_kernel(d_ref, ab_ref, w_ref, r_ref, o_ref, acc_ref, *, scale):
    k = pl.program_id(2)

    @pl.when(k == 0)
    def _():
        acc_ref[...] = jnp.zeros_like(acc_ref)

    acc_ref[...] += _dot(d_ref[...], ab_ref[...])

    @pl.when(k == pl.num_programs(2) - 1)
    def _():
        f = (acc_ref[...] * scale).astype(BF16)
        o_ref[...] = r_ref[...] + _dot(f, w_ref[...])


def dft_out(dft, ab, w_out, res, *, batch, scale, tm=1024, tk=1024):
    S = dft.shape[0]
    D = w_out.shape[0]
    tm, tk = _tile(S, tm), _tile(S, tk)
    nks = S // tk
    nm = S // tm
    return pl.pallas_call(
        functools.partial(_dft_out_kernel, scale=scale),
        grid=(batch, nm, 2 * nks),
        in_specs=[
            pl.BlockSpec((tm, tk), lambda b, m, k: (m, k)),
            pl.BlockSpec((tk, D), lambda b, m, k: (b * nks + k % nks, k // nks)),
            pl.BlockSpec((D, D), lambda b, m, k: (0, 0)),
            pl.BlockSpec((tm, D), lambda b, m, k: (b * nm + m, 0)),
        ],
        out_specs=pl.BlockSpec((tm, D), lambda b, m, k: (b * nm + m, 0)),
        out_shape=jax.ShapeDtypeStruct(res.shape, F32),
        scratch_shapes=[pltpu.VMEM((tm, D), F32)],
        compiler_params=_params("parallel", "parallel", "arbitrary"),
        name="dft_out",
    )(dft, ab, w_out, res)


def _ffn_kernel(x_ref, g_ref, wg_ref, wu_ref, wd_ref, o_ref, xn_ref, acc_ref):
    j = pl.program_id(1)

    @pl.when(j == 0)
    def _():
        xn_ref[...] = _rms(x_ref[...], g_ref[...]).astype(BF16)
        acc_ref[...] = jnp.zeros_like(acc_ref)

    xn = xn_ref[...]
    a = (jax.nn.silu(_dot(xn, wg_ref[...])) * _dot(xn, wu_ref[...])).astype(BF16)
    acc_ref[...] += _dot(a, wd_ref[...])

    @pl.when(j == pl.num_programs(1) - 1)
    def _():
        o_ref[...] = x_ref[...] + acc_ref[...]


def ffn(h, g, wg, wu, wd, *, tm=512, tf=1408):
    M, D = h.shape
    F = wg.shape[1]
    tm, tf = _tile(M, tm), _tile(F, tf)
    return pl.pallas_call(
        _ffn_kernel,
        grid=(M // tm, F // tf),
        in_specs=[
            pl.BlockSpec((tm, D), lambda i, j: (i, 0)),
            pl.BlockSpec((1, D), lambda i, j: (0, 0)),
            pl.BlockSpec((D, tf), lambda i, j: (0, j)),
            pl.BlockSpec((D, tf), lambda i, j: (0, j)),
            pl.BlockSpec((tf, D), lambda i, j: (j, 0)),
        ],
        out_specs=pl.BlockSpec((tm, D), lambda i, j: (i, 0)),
        out_shape=jax.ShapeDtypeStruct((M, D), F32),
        scratch_shapes=[pltpu.VMEM((tm, D), BF16), pltpu.VMEM((tm, D), F32)],
        compiler_params=_params("parallel", "arbitrary"),
        name="ffn",
    )(h, g.reshape(1, D), wg, wu, wd)


def _attn_kernel(ks_ref, tid_ref, q_ref, k_ref, v_ref, b_ref, o_ref, *, n_heads, scale):
    R, W, D = q_ref.shape
    hd = D // n_heads
    nq = R * W
    nk = k_ref.shape[1] * k_ref.shape[2]
    lanes = min(D, max(hd, LANES))
    per_group = lanes // hd
    fold_scale = float(np.frexp(scale)[0]) == 0.5
    lane = lax.broadcasted_iota(I32, (1, lanes), 1)
    for g in range(D // lanes):
        sl = slice(g * lanes, (g + 1) * lanes)
        q = q_ref[:, :, sl].reshape(nq, lanes)
        k = k_ref[0, :, :, sl].reshape(nk, lanes)
        v = v_ref[0, :, :, sl].reshape(nk, lanes)
        if fold_scale:
            q = q * scale
        out = None
        for hh in range(per_group):
            mine = (lane >= hh * hd) & (lane < (hh + 1) * hd)
            qh = jnp.where(mine, q, jnp.zeros_like(q)) if per_group > 1 else q
            s = lax.dot_general(qh, k, (((1,), (1,)), ((), ())), preferred_element_type=F32)
            if not fold_scale:
                s = s * scale
            s = s + b_ref[g * per_group + hh]
            p = jnp.exp(s - jnp.max(s, axis=-1, keepdims=True))
            inv_l = 1.0 / jnp.sum(p, axis=-1, keepdims=True)
            o = _dot(p.astype(BF16), v) * inv_l
            out = o if out is None else jnp.where(mine, o, out)
        o_ref[:, :, sl] = out.reshape(R, W, lanes).astype(o_ref.dtype)


def _attn_plan(rows, kh, R):
    KR = min(rows, R + kh)
    r = np.arange(rows)
    row_start = np.clip(r - kh // 2, 0, rows - kh)
    nblk = rows // R
    ks = np.clip(np.arange(nblk) * R - kh // 2, 0, rows - KR)
    sigs, tid = [], []
    for i in range(nblk):
        rs = row_start[i * R:(i + 1) * R]
        assert (rs >= ks[i]).all() and (rs + kh <= ks[i] + KR).all()
        sig = (i * R - ks[i],) + tuple(rs - ks[i])
        if sig not in sigs:
            sigs.append(sig)
        tid.append(sigs.index(sig))
    return KR, ks.astype(np.int32), np.asarray(tid, np.int32), sigs


def _attn_bias_table(rpb, kh, kw, R, KR, sigs):
    H = rpb.shape[0]
    win_h_max = (rpb.shape[1] + 1) // 2
    W = GRID_W
    c = np.arange(W)
    col_start = np.clip(c - kw // 2, 0, W - kw)
    kc = np.arange(W)
    col_ok = (kc[None, :] >= col_start[:, None]) & (kc[None, :] < col_start[:, None] + kw)
    dc = kc[None, :] - c[:, None] + (kw - 1)
    onehot = (dc[None] == np.arange(2 * kw - 1)[:, None, None]) & col_ok[None]
    tab = jnp.einsum("hrd,dck->hrck", rpb.astype(F32), jnp.asarray(onehot, F32),
                     precision=lax.Precision.HIGHEST)
    blocks, ok = [], []
    for sig in sigs:
        q0, rs = sig[0], sig[1:]
        for rq in range(R):
            for rk in range(KR):
                dr = rk - (q0 + rq) + (win_h_max - 1)
                row_ok = rs[rq] <= rk < rs[rq] + kh
                blocks.append(tab[:, min(max(dr, 0), 2 * win_h_max - 2)])
                ok.append(col_ok & row_ok)
    nt = len(sigs)
    full = jnp.stack(blocks, axis=0).reshape(nt, R, KR, H, W, W)
    ok = np.stack(ok, axis=0).reshape(nt, R, KR, 1, W, W)
    full = jnp.where(jnp.asarray(ok), full, MASK_VALUE)
    full = jnp.transpose(full, (0, 3, 1, 4, 2, 5))
    return full.reshape(nt, H, R * W, KR * W)


def attention(qkv, rpb, *, batch, rows_per_step=4):
    T, D3 = qkv.shape
    D = D3 // 3
    S = T // batch
    W = GRID_W
    rows = S // W
    H = rpb.shape[0]
    kh = min((rpb.shape[1] + 1) // 2, rows)
    kw = (rpb.shape[2] + 1) // 2
    hd = D // H
    R = _tile(rows, rows_per_step)
    KR, ks, tid, sigs = _attn_plan(rows, kh, R)
    bias = _attn_bias_table(rpb, kh, kw, R, KR, sigs)
    qkv4 = qkv.reshape(batch, rows, W, D3)

    def kv_spec(which):
        return pl.BlockSpec(
            (pl.Element(1), pl.Element(KR), pl.Element(W), pl.Element(D)),
            lambda i, b, ks_ref, tid_ref: (b, ks_ref[i], 0, which * D))

    out = pl.pallas_call(
        functools.partial(_attn_kernel, n_heads=H, scale=hd ** -0.5),
        grid_spec=pltpu.PrefetchScalarGridSpec(
            num_scalar_prefetch=2,
            grid=(rows // R, batch),
            in_specs=[
                pl.BlockSpec((None, R, W, D), lambda i, b, ks_ref, tid_ref: (b, i, 0, 0)),
                kv_spec(1),
                kv_spec(2),
                pl.BlockSpec((None, H, R * W, KR * W), lambda i, b, ks_ref, tid_ref: (tid_ref[i], 0, 0, 0)),
            ],
            out_specs=pl.BlockSpec((None, R, W, D), lambda i, b, ks_ref, tid_ref: (b, i, 0, 0)),
        ),
        out_shape=jax.ShapeDtypeStruct((batch, rows, W, D), BF16),
        compiler_params=_params("parallel", "parallel"),
        name="attention",
    )(jnp.asarray(ks), jnp.asarray(tid), qkv4, qkv4, qkv4, bias)
    return out.reshape(T, D)


def _router_kernel(x_ref, g_ref, wr_ref, br_ref, hn_ref, idx_ref, gate_ref):
    hn = _rms(x_ref[...], g_ref[...])
    hn_ref[...] = hn
    logits = jnp.dot(hn, wr_ref[...], preferred_element_type=F32,
                     precision=lax.Precision.HIGHEST) + br_ref[...]
    n_e = logits.shape[1]
    lane = lax.broadcasted_iota(I32, logits.shape, 1)
    v1 = jnp.max(logits, axis=-1, keepdims=True)
    i1 = jnp.min(jnp.where(logits == v1, lane, n_e), axis=-1, keepdims=True)
    rest = jnp.where(lane == i1, -jnp.inf, logits)
    v2 = jnp.max(rest, axis=-1, keepdims=True)
    i2 = jnp.min(jnp.where(rest == v2, lane, n_e), axis=-1, keepdims=True)
    e2 = jnp.exp(v2 - v1)
    first = lax.broadcasted_iota(I32, idx_ref.shape, 1) == 0
    idx_ref[...] = jnp.where(first, i1, i2)
    gate_ref[...] = jnp.where(first, 1.0, e2) / (1.0 + e2)


def router(h, g, wr, br, *, tm=1024):
    M, D = h.shape
    E = wr.shape[1]
    tm = _tile(M, tm)
    return pl.pallas_call(
        _router_kernel,
        grid=(M // tm,),
        in_specs=[
            pl.BlockSpec((tm, D), lambda i: (i, 0)),
            pl.BlockSpec((1, D), lambda i: (0, 0)),
            pl.BlockSpec((D, E), lambda i: (0, 0)),
            pl.BlockSpec((1, E), lambda i: (0, 0)),
        ],
        out_specs=[
            pl.BlockSpec((tm, D), lambda i: (i, 0)),
            pl.BlockSpec((tm, TOP_K), lambda i: (i, 0)),
            pl.BlockSpec((tm, TOP_K), lambda i: (i, 0)),
        ],
        out_shape=[
            jax.ShapeDtypeStruct((M, D), F32),
            jax.ShapeDtypeStruct((M, TOP_K), I32),
            jax.ShapeDtypeStruct((M, TOP_K), F32),
        ],
        compiler_params=_params("parallel"),
        name="router",
    )(h, g.reshape(1, D), wr, br.reshape(1, E))


def _row_copy(src_ref, dst_ref, sem, src_row, dst_row):
    return pltpu.make_async_copy(src_ref.at[pl.ds(src_row, 1), :], dst_ref.at[pl.ds(dst_row, 1), :], sem)


def _dispatch_kernel(dest_ref, hn_ref, xs_in_hbm, xs_hbm, sem):
    del xs_in_hbm
    tq = hn_ref.shape[0]

    def issue(t, carry):
        for k in range(TOP_K):
            _row_copy(hn_ref, xs_hbm, sem, t, dest_ref[TOP_K * t + k]).start()
        return carry

    lax.fori_loop(0, tq, issue, 0)
    for _ in range(TOP_K):
        pltpu.make_async_copy(hn_ref, xs_hbm.at[pl.ds(0, tq), :], sem).wait()


def dispatch_rows(hn, dest, n_rows, *, tq=256):
    T, D = hn.shape
    tq = _tile(T, tq)
    return pl.pallas_call(
        _dispatch_kernel,
        grid=(T // tq,),
        in_specs=[
            pl.BlockSpec((TOP_K * tq,), lambda i: (i,), memory_space=pltpu.SMEM),
            pl.BlockSpec((tq, D), lambda i: (i, 0)),
            pl.BlockSpec(memory_space=pl.ANY),
        ],
        out_specs=pl.BlockSpec(memory_space=pl.ANY),
        out_shape=jax.ShapeDtypeStruct((n_rows, D), hn.dtype),
        scratch_shapes=[pltpu.SemaphoreType.DMA],
        input_output_aliases={2: 0},
        compiler_params=_params("arbitrary"),
        name="dispatch_rows",
    )(dest, hn, jnp.zeros((n_rows, D), hn.dtype))


def _expert_kernel(be_ref, nu_ref, x_ref, wg_ref, wu_ref, wd_ref, o_ref, xb_ref, acc_ref):
    i, j = pl.program_id(0), pl.program_id(1)
    last = pl.num_programs(1) - 1
    used = i < nu_ref[0]

    @pl.when(used)
    def _():
        @pl.when(j == 0)
        def _():
            xb_ref[...] = x_ref[...].astype(BF16)
            acc_ref[...] = jnp.zeros_like(acc_ref)

        x = xb_ref[...]
        a = (jax.nn.silu(_dot(x, wg_ref[...])) * _dot(x, wu_ref[...])).astype(BF16)
        acc_ref[...] += _dot(a, wd_ref[...])

        @pl.when(j == last)
        def _():
            o_ref[...] = acc_ref[...]

    @pl.when(jnp.logical_not(used) & (j == last))
    def _():
        o_ref[...] = jnp.zeros_like(o_ref)


def expert_ffn(xs, block_expert, n_used, wg, wu, wd, *, tm, tf=1792):
    R, D = xs.shape
    F = wg.shape[2]
    tf = _tile(F, tf)
    nf = F // tf

    def fstep(i, j, nu):
        return jnp.where(i < nu[0], j, nf - 1)

    return pl.pallas_call(
        _expert_kernel,
        grid_spec=pltpu.PrefetchScalarGridSpec(
            num_scalar_prefetch=2,
            grid=(R // tm, nf),
            in_specs=[
                pl.BlockSpec((tm, D), lambda i, j, be, nu: (i, 0)),
                pl.BlockSpec((None, D, tf), lambda i, j, be, nu: (be[i], 0, fstep(i, j, nu))),
                pl.BlockSpec((None, D, tf), lambda i, j, be, nu: (be[i], 0, fstep(i, j, nu))),
                pl.BlockSpec((None, tf, D), lambda i, j, be, nu: (be[i], fstep(i, j, nu), 0)),
            ],
            out_specs=pl.BlockSpec((tm, D), lambda i, j, be, nu: (i, 0)),
            scratch_shapes=[pltpu.VMEM((tm, D), BF16), pltpu.VMEM((tm, D), F32)],
        ),
        out_shape=jax.ShapeDtypeStruct((R, D), F32),
        compiler_params=_params("parallel", "arbitrary"),
        name="expert_ffn",
    )(block_expert, n_used, xs, wg, wu, wd)


def _combine_kernel(dest_ref, ys_hbm, h_ref, gate_ref, g_ref, o_ref, buf_ref, sem, *, final_norm):
    tq = h_ref.shape[0]

    def issue(t, carry):
        for k in range(TOP_K):
            _row_copy(ys_hbm, buf_ref, sem, dest_ref[TOP_K * t + k], k * tq + t).start()
        return carry

    lax.fori_loop(0, tq, issue, 0)
    pltpu.make_async_copy(ys_hbm.at[pl.ds(0, TOP_K * tq), :], buf_ref, sem).wait()
    gate = gate_ref[...]
    ff = buf_ref[pl.ds(0, tq), :] * gate[:, 0:1]
    for k in range(1, TOP_K):
        ff = ff + buf_ref[pl.ds(k * tq, tq), :] * gate[:, k:k + 1]
    out = h_ref[...] + ff
    if final_norm:
        out = _rms(out, g_ref[...])
    o_ref[...] = out


def combine(h, ys, dest, gates, final_gain, *, tq=256):
    T, D = h.shape
    tq = _tile(T, tq)
    final_norm = final_gain is not None
    gain = final_gain if final_norm else jnp.ones((D,), F32)
    return pl.pallas_call(
        functools.partial(_combine_kernel, final_norm=final_norm),
        grid=(T // tq,),
        in_specs=[
            pl.BlockSpec((TOP_K * tq,), lambda i: (i,), memory_space=pltpu.SMEM),
            pl.BlockSpec(memory_space=pl.ANY),
            pl.BlockSpec((tq, D), lambda i: (i, 0)),
            pl.BlockSpec((tq, TOP_K), lambda i: (i, 0)),
            pl.BlockSpec((1, D), lambda i: (0, 0)),
        ],
        out_specs=pl.BlockSpec((tq, D), lambda i: (i, 0)),
        out_shape=jax.ShapeDtypeStruct((T, D), F32),
        scratch_shapes=[pltpu.VMEM((TOP_K * tq, D), F32), pltpu.SemaphoreType.DMA],
        compiler_params=_params("arbitrary"),
        name="combine",
    )(dest, ys, h, gates, gain.reshape(1, D))


def _final_norm_kernel(x_ref, g_ref, o_ref):
    o_ref[...] = _rms(x_ref[...], g_ref[...])


def final_norm(h, g, *, tm=1024):
    M, D = h.shape
    tm = _tile(M, tm)
    return pl.pallas_call(
        _final_norm_kernel,
        grid=(M // tm,),
        in_specs=[pl.BlockSpec((tm, D), lambda i: (i, 0)), pl.BlockSpec((1, D), lambda i: (0, 0))],
        out_specs=pl.BlockSpec((tm, D), lambda i: (i, 0)),
        out_shape=jax.ShapeDtypeStruct((M, D), F32),
        compiler_params=_params("parallel"),
        name="final_norm",
    )(h, g.reshape(1, D))


def _dft_cos_sin(n):
    j = lax.broadcasted_iota(I32, (n, n), 0)
    k = lax.broadcasted_iota(I32, (n, n), 1)
    ang = ((j * k) % n).astype(F32) * (2.0 * np.pi / n)
    return jnp.cos(ang), jnp.sin(ang)


def _route(idx, n_experts, tm):
    T = idx.shape[0]
    TK = T * TOP_K
    flat_e = idx.reshape(TK)
    onehot = (flat_e[:, None] == jnp.arange(n_experts, dtype=I32)[None, :]).astype(I32)
    csum = jnp.cumsum(onehot, axis=0)
    rank = jnp.sum(csum * onehot, axis=1) - 1
    counts = csum[-1]
    padded = (counts + tm - 1) // tm * tm
    pend = jnp.cumsum(padded)
    pstart = pend - padded
    dest = (pstart[flat_e] + rank).astype(I32)
    n_blocks = -(-(TK + n_experts * (tm - 1)) // tm)
    block_expert = jnp.minimum(
        jnp.searchsorted(pend, jnp.arange(n_blocks, dtype=I32) * tm, side="right"), n_experts - 1).astype(I32)
    n_used = (pend[-1:] // tm).astype(I32)
    return dest, block_expert, n_used, n_blocks * tm


def _moe_block_rows(T):
    return _tile(T * TOP_K, 512)


def kernel(x, norm_mix, norm_ffn, fourier_w_in, fourier_w_out, nat_w_qkv, nat_rpb, nat_w_o,
           ffn_w_gate, ffn_w_up, ffn_w_down, router_w, router_b,
           expert_w_gate, expert_w_up, expert_w_down, final_norm_g):
    B, S, D = x.shape
    T = B * S
    depth = norm_mix.shape[0]
    n_experts = router_w.shape[2]
    cg = D // N_FOURIER_GROUPS

    cc, sc = _dft_cos_sin(cg)
    eye = jnp.eye(N_FOURIER_GROUPS, dtype=F32)
    chan = jnp.concatenate([jnp.kron(eye, cc), jnp.kron(eye, sc)], axis=1)
    cs, ss = _dft_cos_sin(S)
    seq_dft = jnp.concatenate([cs, -ss], axis=1).astype(BF16)
    dft_scale = float(1.0 / np.sqrt(S * cg))

    h = x.reshape(T, D)
    for i in range(depth):
        j = i // 2
        last = i == depth - 1
        if i % 2 == 0:
            w_ab = matmul_f32(fourier_w_in[j], chan, out_dtype=BF16)
            ab = norm_matmul(h, norm_mix[i], w_ab)
            h = dft_out(seq_dft, ab, fourier_w_out[j].astype(BF16), h, batch=B, scale=dft_scale)
            h = ffn(h, norm_ffn[i], ffn_w_gate[j].astype(BF16), ffn_w_up[j].astype(BF16),
                    ffn_w_down[j].astype(BF16))
            if last:
                h = final_norm(h, final_norm_g)
        else:
            qkv = norm_matmul(h, norm_mix[i], nat_w_qkv[j].astype(BF16))
            att = attention(qkv, nat_rpb[j], batch=B)
            h = matmul_res(att, nat_w_o[j].astype(BF16), h)
            hn, idx, gates = router(h, norm_ffn[i], router_w[j], router_b[j])
            tm = _moe_block_rows(T)
            dest, block_expert, n_used, n_rows = _route(idx, n_experts, tm)
            xs = dispatch_rows(hn, dest, n_rows)
            ys = expert_ffn(xs, block_expert, n_used, expert_w_gate[j].astype(BF16),
                            expert_w_up[j].astype(BF16), expert_w_down[j].astype(BF16), tm=tm)
            h = combine(h, ys, dest, gates, final_norm_g if last else None)
    return h.reshape(B, S, D)
```

```python
import functools

import numpy as np
import jax
import jax.numpy as jnp
from jax import lax
from jax.experimental import pallas as pl
from jax.experimental.pallas import tpu as pltpu

F32 = jnp.float32
BF16 = jnp.bfloat16
I32 = jnp.int32

GRID_W = 64
N_FOURIER_GROUPS = 4
TOP_K = 2
RMS_EPS = 1e-6
MASK_VALUE = -1e30

LANES = 128
SUBLANES = 8

V7X_VMEM_BYTES = 64 * 1024 * 1024
VMEM_LIMIT_BYTES = V7X_VMEM_BYTES - 8 * 1024 * 1024


def _tile(n, target):
    if n <= target:
        return n
    for t in range(target, 0, -1):
        if n % t == 0:
            return t
    return n


def _params(*sem):
    return pltpu.CompilerParams(dimension_semantics=sem, vmem_limit_bytes=VMEM_LIMIT_BYTES)


def _rms(x, g):
    ms = jnp.mean(x * x, axis=-1, keepdims=True)
    return x * lax.rsqrt(ms + RMS_EPS) * g


def _dot(a, b):
    return jnp.dot(a, b, preferred_element_type=F32)


def _norm_matmul_kernel(x_ref, g_ref, w_ref, o_ref, xn_ref):
    @pl.when(pl.program_id(1) == 0)
    def _():
        xn_ref[...] = _rms(x_ref[...], g_ref[...]).astype(BF16)

    o_ref[...] = _dot(xn_ref[...], w_ref[...]).astype(o_ref.dtype)


def norm_matmul(h, g, w, *, tm=1024, tn=1024):
    M, K = h.shape
    N = w.shape[1]
    tm, tn = _tile(M, tm), _tile(N, tn)
    return pl.pallas_call(
        _norm_matmul_kernel,
        grid=(M // tm, N // tn),
        in_specs=[
            pl.BlockSpec((tm, K), lambda i, j: (i, 0)),
            pl.BlockSpec((1, K), lambda i, j: (0, 0)),
            pl.BlockSpec((K, tn), lambda i, j: (0, j)),
        ],
        out_specs=pl.BlockSpec((tm, tn), lambda i, j: (i, j)),
        out_shape=jax.ShapeDtypeStruct((M, N), BF16),
        scratch_shapes=[pltpu.VMEM((tm, K), BF16)],
        compiler_params=_params("parallel", "arbitrary"),
        name="norm_matmul",
    )(h, g.reshape(1, K), w)


def _matmul_res_kernel(a_ref, w_ref, r_ref, o_ref):
    o_ref[...] = r_ref[...] + _dot(a_ref[...], w_ref[...])


def matmul_res(a, w, res, *, tm=1024, tn=1024):
    M, K = a.shape
    N = w.shape[1]
    tm, tn = _tile(M, tm), _tile(N, tn)
    return pl.pallas_call(
        _matmul_res_kernel,
        grid=(M // tm, N // tn),
        in_specs=[
            pl.BlockSpec((tm, K), lambda i, j: (i, 0)),
            pl.BlockSpec((K, tn), lambda i, j: (0, j)),
            pl.BlockSpec((tm, tn), lambda i, j: (i, j)),
        ],
        out_specs=pl.BlockSpec((tm, tn), lambda i, j: (i, j)),
        out_shape=jax.ShapeDtypeStruct((M, N), F32),
        compiler_params=_params("parallel", "parallel"),
        name="matmul_res",
    )(a, w, res)


def _matmul_f32_kernel(a_ref, b_ref, o_ref):
    o_ref[...] = jnp.dot(a_ref[...], b_ref[...], preferred_element_type=F32,
                         precision=lax.Precision.HIGHEST).astype(o_ref.dtype)


def matmul_f32(a, b, *, out_dtype, tm=512, tn=512):
    M, K = a.shape
    N = b.shape[1]
    tm, tn = _tile(M, tm), _tile(N, tn)
    return pl.pallas_call(
        _matmul_f32_kernel,
        grid=(M // tm, N // tn),
        in_specs=[
            pl.BlockSpec((tm, K), lambda i, j: (i, 0)),
            pl.BlockSpec((K, tn), lambda i, j: (0, j)),
        ],
        out_specs=pl.BlockSpec((tm, tn), lambda i, j: (i, j)),
        out_shape=jax.ShapeDtypeStruct((M, N), out_dtype),
        compiler_params=_params("parallel", "parallel"),
        name="matmul_f32",
    )(a, b)


def _dft_np(n):
    j = np.arange(n)
    ang = 2.0 * np.pi * ((j[:, None] * j[None, :]) % n) / n
    return np.cos(ang), np.sin(ang)


def _fft_stage1_kernel(ga_ref, gb_ref, twc_ref, tws_ref, x_ref, o_ref):
    r1 = x_ref.shape[0]
    n_c = twc_ref.shape[0]
    d2 = x_ref.shape[1] // n_c
    d = d2 // 2
    for ci in range(n_c):
        a = x_ref[:, ci * d2:ci * d2 + d]
        b = x_ref[:, ci * d2 + d:(ci + 1) * d2]
        p = _dot(ga_ref[...], a) + _dot(gb_ref[...], b)
        pr, pi = p[:r1], p[r1:]
        tc, ts = twc_ref[ci], tws_ref[ci]
        o_ref[:, ci * d2:ci * d2 + d] = (pr * tc + pi * ts).astype(o_ref.dtype)
        o_ref[:, ci * d2 + d:(ci + 1) * d2] = (pi * tc - pr * ts).astype(o_ref.dtype)


def fft_stage1(ab, *, batch, r1, r2, cols_per_step=4):
    T, D2 = ab.shape
    n_c = _tile(r2, cols_per_step)
    c1, s1 = _dft_np(r1)
    ga = jnp.asarray(np.concatenate([c1, -s1], axis=0), BF16)
    gb = jnp.asarray(np.concatenate([-s1, -c1], axis=0), BF16)
    ang = 2.0 * np.pi * (np.arange(r2)[:, None] * np.arange(r1)[None, :]) / (r1 * r2)
    twc = jnp.asarray(np.cos(ang)[:, :, None], F32)
    tws = jnp.asarray(np.sin(ang)[:, :, None], F32)
    x3 = ab.reshape(batch, r1, r2 * D2)
    out = pl.pallas_call(
        _fft_stage1_kernel,
        grid=(batch, r2 // n_c),
        in_specs=[
            pl.BlockSpec((2 * r1, r1), lambda b, c: (0, 0)),
            pl.BlockSpec((2 * r1, r1), lambda b, c: (0, 0)),
            pl.BlockSpec((n_c, r1, 1), lambda b, c: (c, 0, 0)),
            pl.BlockSpec((n_c, r1, 1), lambda b, c: (c, 0, 0)),
            pl.BlockSpec((None, r1, n_c * D2), lambda b, c: (b, 0, c)),
        ],
        out_specs=pl.BlockSpec((None, r1, n_c * D2), lambda b, c: (b, 0, c)),
        out_shape=jax.ShapeDtypeStruct((batch, r1, r2 * D2), BF16),
        compiler_params=_params("parallel", "parallel"),
        name="fft_stage1",
    )(ga, gb, twc, tws, x3)
    return out.reshape(T, D2)


def _fft_stage2_kernel(c_ref, s_ref, q_ref, w_ref, r_ref, o_ref, f_ref, *, scale):
    kb, r2, d2 = q_ref.shape
    d = d2 // 2
    for j in range(kb):
        f = _dot(c_ref[...], q_ref[j, :, :d]) + _dot(s_ref[...], q_ref[j, :, d:])
        f_ref[j * r2:(j + 1) * r2, :] = (f * scale).astype(BF16)
    g = _dot(f_ref[...], w_ref[...])
    for j in range(kb):
        o_ref[:, j, :] = r_ref[:, j, :] + g[j * r2:(j + 1) * r2, :]


def fft_stage2_out(q, w_out, res, *, batch, r1, r2, scale, kc_per_step=8):
    T, D2 = q.shape
    D = D2 // 2
    kb = _tile(r1, kc_per_step)
    c2, s2 = _dft_np(r2)
    q4 = q.reshape(batch, r1, r2, D2)
    res4 = res.reshape(batch, r2, r1, D)
    out = pl.pallas_call(
        functools.partial(_fft_stage2_kernel, scale=scale),
        grid=(batch, r1 // kb),
        in_specs=[
            pl.BlockSpec((r2, r2), lambda b, k: (0, 0)),
            pl.BlockSpec((r2, r2), lambda b, k: (0, 0)),
            pl.BlockSpec((None, kb, r2, D2), lambda b, k: (b, k, 0, 0)),
            pl.BlockSpec((D, D), lambda b, k: (0, 0)),
            pl.BlockSpec((None, r2, kb, D), lambda b, k: (b, 0, k, 0)),
        ],
        out_specs=pl.BlockSpec((None, r2, kb, D), lambda b, k: (b, 0, k, 0)),
        out_shape=jax.ShapeDtypeStruct((batch, r2, r1, D), F32),
        scratch_shapes=[pltpu.VMEM((kb * r2, D), BF16)],
        compiler_params=_params("parallel", "parallel"),
        name="fft_stage2_out",
    )(jnp.asarray(c2, BF16), jnp.asarray(s2, BF16), q4, w_out, res4)
    return out.reshape(T, D)


def _ffn_kernel(x_ref, g_ref, wg_ref, wu_ref, wd_ref, o_ref, xn_ref, acc_ref):
    j = pl.program_id(1)

    @pl.when(j == 0)
    def _():
        xn_ref[...] = _rms(x_ref[...], g_ref[...]).astype(BF16)
        acc_ref[...] = jnp.zeros_like(acc_ref)

    xn = xn_ref[...]
    a = (jax.nn.silu(_dot(xn, wg_ref[...])) * _dot(xn, wu_ref[...])).astype(BF16)
    acc_ref[...] += _dot(a, wd_ref[...])

    @pl.when(j == pl.num_programs(1) - 1)
    def _():
        o_ref[...] = x_ref[...] + acc_ref[...]


def ffn(h, g, wg, wu, wd, *, tm=512, tf=1408):
    M, D = h.shape
    F = wg.shape[1]
    tm, tf = _tile(M, tm), _tile(F, tf)
    return pl.pallas_call(
        _ffn_kernel,
        grid=(M // tm, F // tf),
        in_specs=[
            pl.BlockSpec((tm, D), lambda i, j: (i, 0)),
            pl.BlockSpec((1, D), lambda i, j: (0, 0)),
            pl.BlockSpec((D, tf), lambda i, j: (0, j)),
            pl.BlockSpec((D, tf), lambda i, j: (0, j)),
            pl.BlockSpec((tf, D), lambda i, j: (j, 0)),
        ],
        out_specs=pl.BlockSpec((tm, D), lambda i, j: (i, 0)),
        out_shape=jax.ShapeDtypeStruct((M, D), F32),
        scratch_shapes=[pltpu.VMEM((tm, D), BF16), pltpu.VMEM((tm, D), F32)],
        compiler_params=_params("parallel", "arbitrary"),
        name="ffn",
    )(h, g.reshape(1, D), wg, wu, wd)


def _attn_kernel(ks_ref, tid_ref, q_ref, k_ref, v_ref, b_ref, o_ref, *, n_heads, scale):
    R, W, D = q_ref.shape
    hd = D // n_heads
    nq = R * W
    nk = k_ref.shape[1] * k_ref.shape[2]
    lanes = min(D, max(hd, LANES))
    per_group = lanes // hd
    fold_scale = float(np.frexp(scale)[0]) == 0.5
    lane = lax.broadcasted_iota(I32, (1, lanes), 1)
    for g in range(D // lanes):
        sl = slice(g * lanes, (g + 1) * lanes)
        q = q_ref[:, :, sl].reshape(nq, lanes)
        k = k_ref[0, :, :, sl].reshape(nk, lanes)
        v = v_ref[0, :, :, sl].reshape(nk, lanes)
        if fold_scale:
            q = q * scale
        out = None
        for hh in range(per_group):
            mine = (lane >= hh * hd) & (lane < (hh + 1) * hd)
            qh = jnp.where(mine, q, jnp.zeros_like(q)) if per_group > 1 else q
            s = lax.dot_general(qh, k, (((1,), (1,)), ((), ())), preferred_element_type=F32)
            if not fold_scale:
                s = s * scale
            s = s + b_ref[g * per_group + hh]
            p = jnp.exp(s - jnp.max(s, axis=-1, keepdims=True))
            inv_l = 1.0 / jnp.sum(p, axis=-1, keepdims=True)
            o = _dot(p.astype(BF16), v) * inv_l
            out = o if out is None else jnp.where(mine, o, out)
        o_ref[:, :, sl] = out.reshape(R, W, lanes).astype(o_ref.dtype)


def _attn_plan(rows, kh, R):
    KR = min(rows, R + kh)
    r = np.arange(rows)
    row_start = np.clip(r - kh // 2, 0, rows - kh)
    nblk = rows // R
    ks = np.clip(np.arange(nblk) * R - kh // 2, 0, rows - KR)
    sigs, tid = [], []
    for i in range(nblk):
        rs = row_start[i * R:(i + 1) * R]
        assert (rs >= ks[i]).all() and (rs + kh <= ks[i] + KR).all()
        sig = (i * R - ks[i],) + tuple(rs - ks[i])
        if sig not in sigs:
            sigs.append(sig)
        tid.append(sigs.index(sig))
    return KR, ks.astype(np.int32), np.asarray(tid, np.int32), sigs


def _attn_bias_table(rpb, kh, kw, R, KR, sigs):
    H = rpb.shape[0]
    win_h_max = (rpb.shape[1] + 1) // 2
    W = GRID_W
    c = np.arange(W)
    col_start = np.clip(c - kw // 2, 0, W - kw)
    kc = np.arange(W)
    col_ok = (kc[None, :] >= col_start[:, None]) & (kc[None, :] < col_start[:, None] + kw)
    dc = kc[None, :] - c[:, None] + (kw - 1)
    onehot = (dc[None] == np.arange(2 * kw - 1)[:, None, None]) & col_ok[None]
    tab = jnp.einsum("hrd,dck->hrck", rpb.astype(F32), jnp.asarray(onehot, F32),
                     precision=lax.Precision.HIGHEST)
    blocks, ok = [], []
    for sig in sigs:
        q0, rs = sig[0], sig[1:]
        for rq in range(R):
            for rk in range(KR):
                dr = rk - (q0 + rq) + (win_h_max - 1)
                row_ok = rs[rq] <= rk < rs[rq] + kh
                blocks.append(tab[:, min(max(dr, 0), 2 * win_h_max - 2)])
                ok.append(col_ok & row_ok)
    nt = len(sigs)
    full = jnp.stack(blocks, axis=0).reshape(nt, R, KR, H, W, W)
    ok = np.stack(ok, axis=0).reshape(nt, R, KR, 1, W, W)
    full = jnp.where(jnp.asarray(ok), full, MASK_VALUE)
    full = jnp.transpose(full, (0, 3, 1, 4, 2, 5))
    return full.reshape(nt, H, R * W, KR * W)


def attention(qkv, rpb, *, batch, rows_per_step=4):
    T, D3 = qkv.shape
    D = D3 // 3
    S = T // batch
    W = GRID_W
    rows = S // W
    H = rpb.shape[0]
    kh = min((rpb.shape[1] + 1) // 2, rows)
    kw = (rpb.shape[2] + 1) // 2
    hd = D // H
    R = _tile(rows, rows_per_step)
    KR, ks, tid, sigs = _attn_plan(rows, kh, R)
    bias = _attn_bias_table(rpb, kh, kw, R, KR, sigs)
    qkv4 = qkv.reshape(batch, rows, W, D3)

    def kv_spec(which):
        return pl.BlockSpec(
            (pl.Element(1), pl.Element(KR), pl.Element(W), pl.Element(D)),
            lambda i, b, ks_ref, tid_ref: (b, ks_ref[i], 0, which * D))

    out = pl.pallas_call(
        functools.partial(_attn_kernel, n_heads=H, scale=hd ** -0.5),
        grid_spec=pltpu.PrefetchScalarGridSpec(
            num_scalar_prefetch=2,
            grid=(rows // R, batch),
            in_specs=[
                pl.BlockSpec((None, R, W, D), lambda i, b, ks_ref, tid_ref: (b, i, 0, 0)),
                kv_spec(1),
                kv_spec(2),
                pl.BlockSpec((None, H, R * W, KR * W), lambda i, b, ks_ref, tid_ref: (tid_ref[i], 0, 0, 0)),
            ],
            out_specs=pl.BlockSpec((None, R, W, D), lambda i, b, ks_ref, tid_ref: (b, i, 0, 0)),
        ),
        out_shape=jax.ShapeDtypeStruct((batch, rows, W, D), BF16),
        compiler_params=_params("parallel", "parallel"),
        name="attention",
    )(jnp.asarray(ks), jnp.asarray(tid), qkv4, qkv4, qkv4, bias)
    return out.reshape(T, D)


def _store_token_tiles(dst_ref, x):
    n, d = x.shape
    s8 = d // LANES
    for s in range(s8):
        dst_ref[pl.ds(s, n, stride=s8), :] = x[:, s * LANES:(s + 1) * LANES]


def _load_token_tiles(src_ref, first, n, s8):
    return jnp.concatenate(
        [src_ref[pl.ds(first * s8 + s, n, stride=s8), :] for s in range(s8)], axis=1)


def _token_copy(src_ref, dst_ref, sem, src_tok, dst_tok, s8):
    return pltpu.make_async_copy(
        src_ref.at[pl.ds(pl.multiple_of(src_tok * s8, s8), s8), :],
        dst_ref.at[pl.ds(pl.multiple_of(dst_tok * s8, s8), s8), :], sem)


def _check_token_tiles(d):
    assert d % (SUBLANES * LANES) == 0, "token-tile layout needs whole (8, 128) tiles per token"
    return d // LANES


def _router_kernel(x_ref, g_ref, wr_ref, br_ref, hn_ref, idx_ref, gate_ref):
    hn = _rms(x_ref[...], g_ref[...])
    _store_token_tiles(hn_ref, hn)
    logits = jnp.dot(hn, wr_ref[...], preferred_element_type=F32,
                     precision=lax.Precision.HIGHEST) + br_ref[...]
    n_e = logits.shape[1]
    lane = lax.broadcasted_iota(I32, logits.shape, 1)
    v1 = jnp.max(logits, axis=-1, keepdims=True)
    i1 = jnp.min(jnp.where(logits == v1, lane, n_e), axis=-1, keepdims=True)
    rest = jnp.where(lane == i1, -jnp.inf, logits)
    v2 = jnp.max(rest, axis=-1, keepdims=True)
    i2 = jnp.min(jnp.where(rest == v2, lane, n_e), axis=-1, keepdims=True)
    e2 = jnp.exp(v2 - v1)
    first = lax.broadcasted_iota(I32, idx_ref.shape, 1) == 0
    idx_ref[...] = jnp.where(first, i1, i2)
    gate_ref[...] = jnp.where(first, 1.0, e2) / (1.0 + e2)


def router(h, g, wr, br, *, tm=1024):
    M, D = h.shape
    E = wr.shape[1]
    s8 = _check_token_tiles(D)
    tm = _tile(M, tm)
    return pl.pallas_call(
        _router_kernel,
        grid=(M // tm,),
        in_specs=[
            pl.BlockSpec((tm, D), lambda i: (i, 0)),
            pl.BlockSpec((1, D), lambda i: (0, 0)),
            pl.BlockSpec((D, E), lambda i: (0, 0)),
            pl.BlockSpec((1, E), lambda i: (0, 0)),
        ],
        out_specs=[
            pl.BlockSpec((tm * s8, LANES), lambda i: (i, 0)),
            pl.BlockSpec((tm, TOP_K), lambda i: (i, 0)),
            pl.BlockSpec((tm, TOP_K), lambda i: (i, 0)),
        ],
        out_shape=[
            jax.ShapeDtypeStruct((M * s8, LANES), F32),
            jax.ShapeDtypeStruct((M, TOP_K), I32),
            jax.ShapeDtypeStruct((M, TOP_K), F32),
        ],
        compiler_params=_params("parallel"),
        name="router",
    )(h, g.reshape(1, D), wr, br.reshape(1, E))


def _dispatch_kernel(dest_ref, hn_ref, xs_in_hbm, xs_hbm, sem, *, s8):
    del xs_in_hbm
    tq = hn_ref.shape[0] // s8

    def issue(t, carry):
        for k in range(TOP_K):
            _token_copy(hn_ref, xs_hbm, sem, t, dest_ref[TOP_K * t + k], s8).start()
        return carry

    lax.fori_loop(0, tq, issue, 0)
    for _ in range(TOP_K):
        pltpu.make_async_copy(hn_ref, xs_hbm.at[pl.ds(0, tq * s8), :], sem).wait()


def dispatch_rows(hn, dest, n_rows, *, s8, tq=512):
    T = hn.shape[0] // s8
    tq = _tile(T, tq)
    return pl.pallas_call(
        functools.partial(_dispatch_kernel, s8=s8),
        grid=(T // tq,),
        in_specs=[
            pl.BlockSpec((TOP_K * tq,), lambda i: (i,), memory_space=pltpu.SMEM),
            pl.BlockSpec((tq * s8, LANES), lambda i: (i, 0)),
            pl.BlockSpec(memory_space=pl.ANY),
        ],
        out_specs=pl.BlockSpec(memory_space=pl.ANY),
        out_shape=jax.ShapeDtypeStruct((n_rows * s8, LANES), hn.dtype),
        scratch_shapes=[pltpu.SemaphoreType.DMA],
        input_output_aliases={2: 0},
        compiler_params=_params("arbitrary"),
        name="dispatch_rows",
    )(dest, hn, jnp.zeros((n_rows * s8, LANES), hn.dtype))


def _expert_kernel(be_ref, nu_ref, x_ref, wg_ref, wu_ref, wd_ref, o_ref, xb_ref, acc_ref):
    i, j = pl.program_id(0), pl.program_id(1)
    last = pl.num_programs(1) - 1
    used = i < nu_ref[0]
    tm, d = xb_ref.shape

    @pl.when(used)
    def _():
        @pl.when(j == 0)
        def _():
            xb_ref[...] = _load_token_tiles(x_ref, 0, tm, d // LANES).astype(BF16)
            acc_ref[...] = jnp.zeros_like(acc_ref)

        x = xb_ref[...]
        a = (jax.nn.silu(_dot(x, wg_ref[...])) * _dot(x, wu_ref[...])).astype(BF16)
        acc_ref[...] += _dot(a, wd_ref[...])

        @pl.when(j == last)
        def _():
            _store_token_tiles(o_ref, acc_ref[...])

    @pl.when(jnp.logical_not(used) & (j == last))
    def _():
        o_ref[...] = jnp.zeros_like(o_ref)


def expert_ffn(xs, block_expert, n_used, wg, wu, wd, *, tm, tf=1792):
    D, F = wg.shape[1], wg.shape[2]
    s8 = D // LANES
    R = xs.shape[0] // s8
    tf = _tile(F, tf)
    nf = F // tf

    def fstep(i, j, nu):
        return jnp.where(i < nu[0], j, nf - 1)

    return pl.pallas_call(
        _expert_kernel,
        grid_spec=pltpu.PrefetchScalarGridSpec(
            num_scalar_prefetch=2,
            grid=(R // tm, nf),
            in_specs=[
                pl.BlockSpec((tm * s8, LANES), lambda i, j, be, nu: (i, 0)),
                pl.BlockSpec((None, D, tf), lambda i, j, be, nu: (be[i], 0, fstep(i, j, nu))),
                pl.BlockSpec((None, D, tf), lambda i, j, be, nu: (be[i], 0, fstep(i, j, nu))),
                pl.BlockSpec((None, tf, D), lambda i, j, be, nu: (be[i], fstep(i, j, nu), 0)),
            ],
            out_specs=pl.BlockSpec((tm * s8, LANES), lambda i, j, be, nu: (i, 0)),
            scratch_shapes=[pltpu.VMEM((tm, D), BF16), pltpu.VMEM((tm, D), F32)],
        ),
        out_shape=jax.ShapeDtypeStruct((R * s8, LANES), F32),
        compiler_params=_params("parallel", "arbitrary"),
        name="expert_ffn",
    )(block_expert, n_used, xs, wg, wu, wd)


def _combine_kernel(dest_ref, ys_hbm, h_ref, gate_ref, g_ref, o_ref, buf_ref, sem, *, final_norm):
    tq, d = h_ref.shape
    s8 = d // LANES

    def issue(t, carry):
        for k in range(TOP_K):
            _token_copy(ys_hbm, buf_ref, sem, dest_ref[TOP_K * t + k], k * tq + t, s8).start()
        return carry

    lax.fori_loop(0, tq, issue, 0)
    pltpu.make_async_copy(ys_hbm.at[pl.ds(0, TOP_K * tq * s8), :], buf_ref, sem).wait()
    gate = gate_ref[...]
    ff = _load_token_tiles(buf_ref, 0, tq, s8) * gate[:, 0:1]
    for k in range(1, TOP_K):
        ff = ff + _load_token_tiles(buf_ref, k * tq, tq, s8) * gate[:, k:k + 1]
    out = h_ref[...] + ff
    if final_norm:
        out = _rms(out, g_ref[...])
    o_ref[...] = out


def combine(h, ys, dest, gates, final_gain, *, tq=512):
    T, D = h.shape
    s8 = D // LANES
    tq = _tile(T, tq)
    final_norm = final_gain is not None
    gain = final_gain if final_norm else jnp.ones((D,), F32)
    return pl.pallas_call(
        functools.partial(_combine_kernel, final_norm=final_norm),
        grid=(T // tq,),
        in_specs=[
            pl.BlockSpec((TOP_K * tq,), lambda i: (i,), memory_space=pltpu.SMEM),
            pl.BlockSpec(memory_space=pl.ANY),
            pl.BlockSpec((tq, D), lambda i: (i, 0)),
            pl.BlockSpec((tq, TOP_K), lambda i: (i, 0)),
            pl.BlockSpec((1, D), lambda i: (0, 0)),
        ],
        out_specs=pl.BlockSpec((tq, D), lambda i: (i, 0)),
        out_shape=jax.ShapeDtypeStruct((T, D), F32),
        scratch_shapes=[pltpu.VMEM((TOP_K * tq * s8, LANES), F32), pltpu.SemaphoreType.DMA],
        compiler_params=_params("arbitrary"),
        name="combine",
    )(dest, ys, h, gates, gain.reshape(1, D))


def _final_norm_kernel(x_ref, g_ref, o_ref):
    o_ref[...] = _rms(x_ref[...], g_ref[...])


def final_norm(h, g, *, tm=1024):
    M, D = h.shape
    tm = _tile(M, tm)
    return pl.pallas_call(
        _final_norm_kernel,
        grid=(M // tm,),
        in_specs=[pl.BlockSpec((tm, D), lambda i: (i, 0)), pl.BlockSpec((1, D), lambda i: (0, 0))],
        out_specs=pl.BlockSpec((tm, D), lambda i: (i, 0)),
        out_shape=jax.ShapeDtypeStruct((M, D), F32),
        compiler_params=_params("parallel"),
        name="final_norm",
    )(h, g.reshape(1, D))


def _dft_cos_sin(n):
    j = lax.broadcasted_iota(I32, (n, n), 0)
    k = lax.broadcasted_iota(I32, (n, n), 1)
    ang = ((j * k) % n).astype(F32) * (2.0 * np.pi / n)
    return jnp.cos(ang), jnp.sin(ang)


def _route(idx, n_experts, tm):
    T = idx.shape[0]
    TK = T * TOP_K
    flat_e = idx.reshape(TK)
    onehot = (flat_e[:, None] == jnp.arange(n_experts, dtype=I32)[None, :]).astype(I32)
    csum = jnp.cumsum(onehot, axis=0)
    rank = jnp.sum(csum * onehot, axis=1) - 1
    counts = csum[-1]
    padded = (counts + tm - 1) // tm * tm
    pend = jnp.cumsum(padded)
    pstart = pend - padded
    dest = (pstart[flat_e] + rank).astype(I32)
    n_blocks = -(-(TK + n_experts * (tm - 1)) // tm)
    block_expert = jnp.minimum(
        jnp.searchsorted(pend, jnp.arange(n_blocks, dtype=I32) * tm, side="right"), n_experts - 1).astype(I32)
    n_used = (pend[-1:] // tm).astype(I32)
    return dest, block_expert, n_used, n_blocks * tm


def _moe_block_rows(T):
    return _tile(T * TOP_K, 512)


def kernel(x, norm_mix, norm_ffn, fourier_w_in, fourier_w_out, nat_w_qkv, nat_rpb, nat_w_o,
           ffn_w_gate, ffn_w_up, ffn_w_down, router_w, router_b,
           expert_w_gate, expert_w_up, expert_w_down, final_norm_g):
    B, S, D = x.shape
    T = B * S
    depth = norm_mix.shape[0]
    n_experts = router_w.shape[2]
    cg = D // N_FOURIER_GROUPS

    cc, sc = _dft_cos_sin(cg)
    eye = jnp.eye(N_FOURIER_GROUPS, dtype=F32)
    chan = jnp.concatenate([jnp.kron(eye, cc), jnp.kron(eye, sc)], axis=1)
    dft_scale = float(1.0 / np.sqrt(S * cg))
    r1, r2 = S // GRID_W, GRID_W

    h = x.reshape(T, D)
    for i in range(depth):
        j = i // 2
        last = i == depth - 1
        if i % 2 == 0:
            w_ab = matmul_f32(fourier_w_in[j], chan, out_dtype=BF16)
            ab = norm_matmul(h, norm_mix[i], w_ab)
            q = fft_stage1(ab, batch=B, r1=r1, r2=r2)
            h = fft_stage2_out(q, fourier_w_out[j].astype(BF16), h, batch=B, r1=r1, r2=r2, scale=dft_scale)
            h = ffn(h, norm_ffn[i], ffn_w_gate[j].astype(BF16), ffn_w_up[j].astype(BF16),
                    ffn_w_down[j].astype(BF16))
            if last:
                h = final_norm(h, final_norm_g)
        else:
            qkv = norm_matmul(h, norm_mix[i], nat_w_qkv[j].astype(BF16))
            att = attention(qkv, nat_rpb[j], batch=B)
            h = matmul_res(att, nat_w_o[j].astype(BF16), h)
            hn, idx, gates = router(h, norm_ffn[i], router_w[j], router_b[j])
            tm = _moe_block_rows(T)
            dest, block_expert, n_used, n_rows = _route(idx, n_experts, tm)
            xs = dispatch_rows(hn, dest, n_rows, s8=D // LANES)
            ys = expert_ffn(xs, block_expert, n_used, expert_w_gate[j].astype(BF16),
                            expert_w_up[j].astype(BF16), expert_w_down[j].astype(BF16), tm=tm)
            h = combine(h, ys, dest, gates, final_norm_g if last else None)
    return h.reshape(B, S, D)
```

```python
import functools

import numpy as np
import jax
import jax.numpy as jnp
from jax import lax
from jax.experimental import pallas as pl
from jax.experimental.pallas import tpu as pltpu

F32 = jnp.float32
BF16 = jnp.bfloat16
I32 = jnp.int32

GRID_W = 64
N_FOURIER_GROUPS = 4
TOP_K = 2
RMS_EPS = 1e-6
MASK_VALUE = -1e30

LANES = 128
SUBLANES = 8

V7X_VMEM_BYTES = 64 * 1024 * 1024
VMEM_LIMIT_BYTES = V7X_VMEM_BYTES - 8 * 1024 * 1024


def _tile(n, target):
    if n <= target:
        return n
    for t in range(target, 0, -1):
        if n % t == 0:
            return t
    return n


def _params(*sem):
    return pltpu.CompilerParams(dimension_semantics=sem, vmem_limit_bytes=VMEM_LIMIT_BYTES)


def _rms(x, g):
    ms = jnp.mean(x * x, axis=-1, keepdims=True)
    return x * lax.rsqrt(ms + RMS_EPS) * g


def _dot(a, b):
    return jnp.dot(a, b, preferred_element_type=F32)


def _norm_matmul_kernel(x_ref, g_ref, w_ref, o_ref, xn_ref):
    @pl.when(pl.program_id(1) == 0)
    def _():
        xn_ref[...] = _rms(x_ref[...], g_ref[...]).astype(BF16)

    o_ref[...] = _dot(xn_ref[...], w_ref[...]).astype(o_ref.dtype)


def norm_matmul(h, g, w, *, tm=1024, tn=1024):
    M, K = h.shape
    N = w.shape[1]
    tm, tn = _tile(M, tm), _tile(N, tn)
    return pl.pallas_call(
        _norm_matmul_kernel,
        grid=(M // tm, N // tn),
        in_specs=[
            pl.BlockSpec((tm, K), lambda i, j: (i, 0)),
            pl.BlockSpec((1, K), lambda i, j: (0, 0)),
            pl.BlockSpec((K, tn), lambda i, j: (0, j)),
        ],
        out_specs=pl.BlockSpec((tm, tn), lambda i, j: (i, j)),
        out_shape=jax.ShapeDtypeStruct((M, N), BF16),
        scratch_shapes=[pltpu.VMEM((tm, K), BF16)],
        compiler_params=_params("parallel", "arbitrary"),
        name="norm_matmul",
    )(h, g.reshape(1, K), w)


def _matmul_res_kernel(a_ref, w_ref, r_ref, o_ref):
    o_ref[...] = r_ref[...] + _dot(a_ref[...], w_ref[...])


def matmul_res(a, w, res, *, tm=1024, tn=1024):
    M, K = a.shape
    N = w.shape[1]
    tm, tn = _tile(M, tm), _tile(N, tn)
    return pl.pallas_call(
        _matmul_res_kernel,
        grid=(M // tm, N // tn),
        in_specs=[
            pl.BlockSpec((tm, K), lambda i, j: (i, 0)),
            pl.BlockSpec((K, tn), lambda i, j: (0, j)),
            pl.BlockSpec((tm, tn), lambda i, j: (i, j)),
        ],
        out_specs=pl.BlockSpec((tm, tn), lambda i, j: (i, j)),
        out_shape=jax.ShapeDtypeStruct((M, N), F32),
        compiler_params=_params("parallel", "parallel"),
        name="matmul_res",
    )(a, w, res)


def _matmul_f32_kernel(a_ref, b_ref, o_ref):
    o_ref[...] = jnp.dot(a_ref[...], b_ref[...], preferred_element_type=F32,
                         precision=lax.Precision.HIGHEST).astype(o_ref.dtype)


def matmul_f32(a, b, *, out_dtype, tm=512, tn=512):
    M, K = a.shape
    N = b.shape[1]
    tm, tn = _tile(M, tm), _tile(N, tn)
    return pl.pallas_call(
        _matmul_f32_kernel,
        grid=(M // tm, N // tn),
        in_specs=[
            pl.BlockSpec((tm, K), lambda i, j: (i, 0)),
            pl.BlockSpec((K, tn), lambda i, j: (0, j)),
        ],
        out_specs=pl.BlockSpec((tm, tn), lambda i, j: (i, j)),
        out_shape=jax.ShapeDtypeStruct((M, N), out_dtype),
        compiler_params=_params("parallel", "parallel"),
        name="matmul_f32",
    )(a, b)


def _dft_np(n):
    j = np.arange(n)
    ang = 2.0 * np.pi * ((j[:, None] * j[None, :]) % n) / n
    return np.cos(ang), np.sin(ang)


def _fourier_in_kernel(x_ref, g_ref, w_ref, ga_ref, gb_ref, twc_ref, tws_ref, o_ref):
    r1, n_c, dm = x_ref.shape
    n = r1 * n_c
    xn = _rms(x_ref[...].reshape(n, dm), g_ref[...]).astype(BF16)
    ab = _dot(xn, w_ref[...])
    d = ab.shape[1] // 2
    a, b = ab[:, :d].astype(BF16), ab[:, d:].astype(BF16)
    p = _dot(ga_ref[...], a) + _dot(gb_ref[...], b)
    pr, pi = p[:n], p[n:]
    tc, ts = twc_ref[...], tws_ref[...]
    q = jnp.concatenate([pr * tc + pi * ts, pi * tc - pr * ts], axis=1)
    o_ref[...] = q.reshape(r1, n_c, 2 * d)


def fourier_in(h, g, w_ab, *, batch, r1, r2, cols_per_step=SUBLANES):
    T, D = h.shape
    D2 = w_ab.shape[1]
    n_c = _tile(r2, cols_per_step)
    n = r1 * n_c
    c1, s1 = _dft_np(r1)
    eye = np.eye(n_c)
    ga = jnp.asarray(np.kron(np.concatenate([c1, -s1], axis=0), eye), BF16)
    gb = jnp.asarray(np.kron(np.concatenate([-s1, -c1], axis=0), eye), BF16)
    c = np.arange(r2).reshape(r2 // n_c, 1, n_c)
    kc = np.arange(r1).reshape(1, r1, 1)
    ang = (2.0 * np.pi * (c * kc) / (r1 * r2)).reshape(r2 // n_c, n, 1)
    twc, tws = jnp.asarray(np.cos(ang), F32), jnp.asarray(np.sin(ang), F32)
    out = pl.pallas_call(
        _fourier_in_kernel,
        grid=(batch, r2 // n_c),
        in_specs=[
            pl.BlockSpec((None, r1, n_c, D), lambda b, c: (b, 0, c, 0)),
            pl.BlockSpec((1, D), lambda b, c: (0, 0)),
            pl.BlockSpec((D, D2), lambda b, c: (0, 0)),
            pl.BlockSpec((2 * n, n), lambda b, c: (0, 0)),
            pl.BlockSpec((2 * n, n), lambda b, c: (0, 0)),
            pl.BlockSpec((None, n, 1), lambda b, c: (c, 0, 0)),
            pl.BlockSpec((None, n, 1), lambda b, c: (c, 0, 0)),
        ],
        out_specs=pl.BlockSpec((None, r1, n_c, D2), lambda b, c: (b, 0, c, 0)),
        out_shape=jax.ShapeDtypeStruct((batch, r1, r2, D2), F32),
        compiler_params=_params("parallel", "parallel"),
        name="fourier_in",
    )(h.reshape(batch, r1, r2, D), g.reshape(1, D), w_ab, ga, gb, twc, tws)
    return out.reshape(T, D2)


def _fft_stage2_kernel(c_ref, s_ref, q_ref, w_ref, r_ref, o_ref, f_ref, *, scale):
    kb, r2, d2 = q_ref.shape
    d = d2 // 2
    for j in range(kb):
        qr, qi = q_ref[j, :, :d].astype(BF16), q_ref[j, :, d:].astype(BF16)
        f = _dot(c_ref[...], qr) + _dot(s_ref[...], qi)
        f_ref[j * r2:(j + 1) * r2, :] = (f * scale).astype(BF16)
    g = _dot(f_ref[...], w_ref[...])
    for j in range(kb):
        o_ref[:, j, :] = r_ref[:, j, :] + g[j * r2:(j + 1) * r2, :]


def fft_stage2_out(q, w_out, res, *, batch, r1, r2, scale, kc_per_step=8):
    T, D2 = q.shape
    D = D2 // 2
    kb = _tile(r1, kc_per_step)
    c2, s2 = _dft_np(r2)
    q4 = q.reshape(batch, r1, r2, D2)
    res4 = res.reshape(batch, r2, r1, D)
    out = pl.pallas_call(
        functools.partial(_fft_stage2_kernel, scale=scale),
        grid=(batch, r1 // kb),
        in_specs=[
            pl.BlockSpec((r2, r2), lambda b, k: (0, 0)),
            pl.BlockSpec((r2, r2), lambda b, k: (0, 0)),
            pl.BlockSpec((None, kb, r2, D2), lambda b, k: (b, k, 0, 0)),
            pl.BlockSpec((D, D), lambda b, k: (0, 0)),
            pl.BlockSpec((None, r2, kb, D), lambda b, k: (b, 0, k, 0)),
        ],
        out_specs=pl.BlockSpec((None, r2, kb, D), lambda b, k: (b, 0, k, 0)),
        out_shape=jax.ShapeDtypeStruct((batch, r2, r1, D), F32),
        scratch_shapes=[pltpu.VMEM((kb * r2, D), BF16)],
        compiler_params=_params("parallel", "parallel"),
        name="fft_stage2_out",
    )(jnp.asarray(c2, BF16), jnp.asarray(s2, BF16), q4, w_out, res4)
    return out.reshape(T, D)


def _ffn_kernel(x_ref, g_ref, wg_ref, wu_ref, wd_ref, o_ref, xn_ref, acc_ref):
    j = pl.program_id(1)

    @pl.when(j == 0)
    def _():
        xn_ref[...] = _rms(x_ref[...], g_ref[...]).astype(BF16)
        acc_ref[...] = jnp.zeros_like(acc_ref)

    xn = xn_ref[...]
    a = (jax.nn.silu(_dot(xn, wg_ref[...])) * _dot(xn, wu_ref[...])).astype(BF16)
    acc_ref[...] += _dot(a, wd_ref[...])

    @pl.when(j == pl.num_programs(1) - 1)
    def _():
        o_ref[...] = x_ref[...] + acc_ref[...]


def ffn(h, g, wg, wu, wd, *, tm=512, tf=1408):
    M, D = h.shape
    F = wg.shape[1]
    tm, tf = _tile(M, tm), _tile(F, tf)
    return pl.pallas_call(
        _ffn_kernel,
        grid=(M // tm, F // tf),
        in_specs=[
            pl.BlockSpec((tm, D), lambda i, j: (i, 0)),
            pl.BlockSpec((1, D), lambda i, j: (0, 0)),
            pl.BlockSpec((D, tf), lambda i, j: (0, j)),
            pl.BlockSpec((D, tf), lambda i, j: (0, j)),
            pl.BlockSpec((tf, D), lambda i, j: (j, 0)),
        ],
        out_specs=pl.BlockSpec((tm, D), lambda i, j: (i, 0)),
        out_shape=jax.ShapeDtypeStruct((M, D), F32),
        scratch_shapes=[pltpu.VMEM((tm, D), BF16), pltpu.VMEM((tm, D), F32)],
        compiler_params=_params("parallel", "arbitrary"),
        name="ffn",
    )(h, g.reshape(1, D), wg, wu, wd)


def _attn_kernel(ks_ref, tid_ref, dr_ref, q_ref, k_ref, v_ref, tab_ref, o_ref, b_ref, *, n_heads, scale):
    R, W, D = q_ref.shape
    KR = k_ref.shape[1]
    hd = D // n_heads
    nq = R * W
    nk = KR * W

    i, bidx = pl.program_id(0), pl.program_id(1)
    tid = tid_ref[i]
    changed = (i == 0) | (tid != tid_ref[jnp.maximum(i - 1, 0)])

    @pl.when((bidx == 0) & changed)
    def _():
        def per_head(h, carry):
            for rq in range(R):
                for rk in range(KR):
                    dr = dr_ref[(tid * R + rq) * KR + rk]
                    b_ref[h, rq * W:(rq + 1) * W, rk * W:(rk + 1) * W] = tab_ref[h, dr]
            return carry

        lax.fori_loop(0, n_heads, per_head, 0)

    lanes = min(D, max(hd, LANES))
    per_group = lanes // hd
    fold_scale = float(np.frexp(scale)[0]) == 0.5
    lane = lax.broadcasted_iota(I32, (1, lanes), 1)
    for g in range(D // lanes):
        sl = slice(g * lanes, (g + 1) * lanes)
        q = q_ref[:, :, sl].reshape(nq, lanes)
        k = k_ref[0, :, :, sl].reshape(nk, lanes)
        v = v_ref[0, :, :, sl].reshape(nk, lanes)
        if fold_scale:
            q = q * scale
        out = None
        for hh in range(per_group):
            mine = (lane >= hh * hd) & (lane < (hh + 1) * hd)
            qh = jnp.where(mine, q, jnp.zeros_like(q)) if per_group > 1 else q
            s = lax.dot_general(qh, k, (((1,), (1,)), ((), ())), preferred_element_type=F32)
            if not fold_scale:
                s = s * scale
            s = s + b_ref[g * per_group + hh]
            p = jnp.exp(s - jnp.max(s, axis=-1, keepdims=True))
            inv_l = 1.0 / jnp.sum(p, axis=-1, keepdims=True)
            o = _dot(p.astype(BF16), v) * inv_l
            out = o if out is None else jnp.where(mine, o, out)
        o_ref[:, :, sl] = out.reshape(R, W, lanes).astype(o_ref.dtype)


def _attn_plan(rows, kh, R):
    KR = min(rows, R + kh)
    r = np.arange(rows)
    row_start = np.clip(r - kh // 2, 0, rows - kh)
    nblk = rows // R
    ks = np.clip(np.arange(nblk) * R - kh // 2, 0, rows - KR)
    sigs, tid = [], []
    for i in range(nblk):
        rs = row_start[i * R:(i + 1) * R]
        assert (rs >= ks[i]).all() and (rs + kh <= ks[i] + KR).all()
        sig = (i * R - ks[i],) + tuple(rs - ks[i])
        if sig not in sigs:
            sigs.append(sig)
        tid.append(sigs.index(sig))
    return KR, ks.astype(np.int32), np.asarray(tid, np.int32), sigs


def _attn_bias_table(rpb, kh, kw, R, KR, sigs):
    H = rpb.shape[0]
    win_h_max = (rpb.shape[1] + 1) // 2
    W = GRID_W
    c = np.arange(W)
    col_start = np.clip(c - kw // 2, 0, W - kw)
    kc = np.arange(W)
    col_ok = (kc[None, :] >= col_start[:, None]) & (kc[None, :] < col_start[:, None] + kw)
    dc = kc[None, :] - c[:, None] + (kw - 1)
    onehot = (dc[None] == np.arange(2 * kw - 1)[:, None, None]) & col_ok[None]
    tab = jnp.einsum("hrd,dck->hrck", rpb.astype(F32), jnp.asarray(onehot, F32),
                     precision=lax.Precision.HIGHEST)
    tab = jnp.where(jnp.asarray(col_ok), tab, MASK_VALUE)
    n_dr = 2 * win_h_max - 1
    tab = jnp.concatenate([tab, jnp.full((H, 1, W, W), MASK_VALUE, F32)], axis=1)
    dr_of = np.full((len(sigs), R, KR), n_dr, np.int32)
    for t, sig in enumerate(sigs):
        q0, rs = sig[0], sig[1:]
        for rq in range(R):
            for rk in range(KR):
                if rs[rq] <= rk < rs[rq] + kh:
                    dr_of[t, rq, rk] = rk - (q0 + rq) + (win_h_max - 1)
    assert dr_of.min() >= 0
    return tab, dr_of.reshape(-1)


def attention(qkv, rpb, *, batch, rows_per_step=4):
    T, D3 = qkv.shape
    D = D3 // 3
    S = T // batch
    W = GRID_W
    rows = S // W
    H = rpb.shape[0]
    kh = min((rpb.shape[1] + 1) // 2, rows)
    kw = (rpb.shape[2] + 1) // 2
    hd = D // H
    R = _tile(rows, rows_per_step)
    KR, ks, tid, sigs = _attn_plan(rows, kh, R)
    tab, dr_of = _attn_bias_table(rpb, kh, kw, R, KR, sigs)
    qkv4 = qkv.reshape(batch, rows, W, D3)

    def kv_spec(which):
        return pl.BlockSpec(
            (pl.Element(1), pl.Element(KR), pl.Element(W), pl.Element(D)),
            lambda i, b, ks_ref, tid_ref, dr_ref: (b, ks_ref[i], 0, which * D))

    out = pl.pallas_call(
        functools.partial(_attn_kernel, n_heads=H, scale=hd ** -0.5),
        grid_spec=pltpu.PrefetchScalarGridSpec(
            num_scalar_prefetch=3,
            grid=(rows // R, batch),
            in_specs=[
                pl.BlockSpec((None, R, W, D), lambda i, b, ks_ref, tid_ref, dr_ref: (b, i, 0, 0)),
                kv_spec(1),
                kv_spec(2),
                pl.BlockSpec(tab.shape, lambda i, b, ks_ref, tid_ref, dr_ref: (0, 0, 0, 0)),
            ],
            out_specs=pl.BlockSpec((None, R, W, D), lambda i, b, ks_ref, tid_ref, dr_ref: (b, i, 0, 0)),
            scratch_shapes=[pltpu.VMEM((H, R * W, KR * W), F32)],
        ),
        out_shape=jax.ShapeDtypeStruct((batch, rows, W, D), BF16),
        compiler_params=_params("arbitrary", "arbitrary"),
        name="attention",
    )(jnp.asarray(ks), jnp.asarray(tid), jnp.asarray(dr_of), qkv4, qkv4, qkv4, tab)
    return out.reshape(T, D)


def _store_token_tiles(dst_ref, x):
    n, d = x.shape
    s8 = d // LANES
    for s in range(s8):
        dst_ref[pl.ds(s, n, stride=s8), :] = x[:, s * LANES:(s + 1) * LANES]


def _load_token_tiles(src_ref, first, n, s8):
    return jnp.concatenate(
        [src_ref[pl.ds(first * s8 + s, n, stride=s8), :] for s in range(s8)], axis=1)


def _token_copy(src_ref, dst_ref, sem, src_tok, dst_tok, s8):
    return pltpu.make_async_copy(
        src_ref.at[pl.ds(pl.multiple_of(src_tok * s8, s8), s8), :],
        dst_ref.at[pl.ds(pl.multiple_of(dst_tok * s8, s8), s8), :], sem)


def _check_token_tiles(d):
    assert d % (SUBLANES * LANES) == 0, "token-tile layout needs whole (8, 128) tiles per token"
    return d // LANES


def _router_kernel(x_ref, g_ref, wr_ref, br_ref, hn_ref, idx_ref, gate_ref):
    hn = _rms(x_ref[...], g_ref[...])
    _store_token_tiles(hn_ref, hn)
    logits = jnp.dot(hn, wr_ref[...], preferred_element_type=F32,
                     precision=lax.Precision.HIGHEST) + br_ref[...]
    n_e = logits.shape[1]
    lane = lax.broadcasted_iota(I32, logits.shape, 1)
    v1 = jnp.max(logits, axis=-1, keepdims=True)
    i1 = jnp.min(jnp.where(logits == v1, lane, n_e), axis=-1, keepdims=True)
    rest = jnp.where(lane == i1, -jnp.inf, logits)
    v2 = jnp.max(rest, axis=-1, keepdims=True)
    i2 = jnp.min(jnp.where(rest == v2, lane, n_e), axis=-1, keepdims=True)
    e2 = jnp.exp(v2 - v1)
    first = lax.broadcasted_iota(I32, idx_ref.shape, 1) == 0
    idx_ref[...] = jnp.where(first, i1, i2)
    gate_ref[...] = jnp.where(first, 1.0, e2) / (1.0 + e2)


def router(h, g, wr, br, *, tm=1024):
    M, D = h.shape
    E = wr.shape[1]
    s8 = _check_token_tiles(D)
    tm = _tile(M, tm)
    return pl.pallas_call(
        _router_kernel,
        grid=(M // tm,),
        in_specs=[
            pl.BlockSpec((tm, D), lambda i: (i, 0)),
            pl.BlockSpec((1, D), lambda i: (0, 0)),
            pl.BlockSpec((D, E), lambda i: (0, 0)),
            pl.BlockSpec((1, E), lambda i: (0, 0)),
        ],
        out_specs=[
            pl.BlockSpec((tm * s8, LANES), lambda i: (i, 0)),
            pl.BlockSpec((tm, TOP_K), lambda i: (i, 0)),
            pl.BlockSpec((tm, TOP_K), lambda i: (i, 0)),
        ],
        out_shape=[
            jax.ShapeDtypeStruct((M * s8, LANES), F32),
            jax.ShapeDtypeStruct((M, TOP_K), I32),
            jax.ShapeDtypeStruct((M, TOP_K), F32),
        ],
        compiler_params=_params("parallel"),
        name="router",
    )(h, g.reshape(1, D), wr, br.reshape(1, E))


def _dispatch_kernel(dest_ref, hn_ref, xs_in_hbm, xs_hbm, sem, *, s8):
    del xs_in_hbm
    tq = hn_ref.shape[0] // s8

    def issue(t, carry):
        for k in range(TOP_K):
            _token_copy(hn_ref, xs_hbm, sem, t, dest_ref[TOP_K * t + k], s8).start()
        return carry

    lax.fori_loop(0, tq, issue, 0)
    for _ in range(TOP_K):
        pltpu.make_async_copy(hn_ref, xs_hbm.at[pl.ds(0, tq * s8), :], sem).wait()


def dispatch_rows(hn, dest, n_rows, *, s8, tq=512):
    T = hn.shape[0] // s8
    tq = _tile(T, tq)
    return pl.pallas_call(
        functools.partial(_dispatch_kernel, s8=s8),
        grid=(T // tq,),
        in_specs=[
            pl.BlockSpec((TOP_K * tq,), lambda i: (i,), memory_space=pltpu.SMEM),
            pl.BlockSpec((tq * s8, LANES), lambda i: (i, 0)),
            pl.BlockSpec(memory_space=pl.ANY),
        ],
        out_specs=pl.BlockSpec(memory_space=pl.ANY),
        out_shape=jax.ShapeDtypeStruct((n_rows * s8, LANES), hn.dtype),
        scratch_shapes=[pltpu.SemaphoreType.DMA],
        input_output_aliases={2: 0},
        compiler_params=_params("arbitrary"),
        name="dispatch_rows",
    )(dest, hn, jnp.zeros((n_rows * s8, LANES), hn.dtype))


def _expert_kernel(be_ref, nu_ref, x_ref, wg_ref, wu_ref, wd_ref, o_ref, xb_ref, acc_ref):
    i, j = pl.program_id(0), pl.program_id(1)
    last = pl.num_programs(1) - 1
    used = i < nu_ref[0]
    tm, d = xb_ref.shape

    @pl.when(used)
    def _():
        @pl.when(j == 0)
        def _():
            xb_ref[...] = _load_token_tiles(x_ref, 0, tm, d // LANES).astype(BF16)
            acc_ref[...] = jnp.zeros_like(acc_ref)

        x = xb_ref[...]
        a = (jax.nn.silu(_dot(x, wg_ref[...])) * _dot(x, wu_ref[...])).astype(BF16)
        acc_ref[...] += _dot(a, wd_ref[...])

        @pl.when(j == last)
        def _():
            _store_token_tiles(o_ref, acc_ref[...])

    @pl.when(jnp.logical_not(used) & (j == last))
    def _():
        o_ref[...] = jnp.zeros_like(o_ref)


def expert_ffn(xs, block_expert, n_used, wg, wu, wd, *, tm, tf=1792):
    D, F = wg.shape[1], wg.shape[2]
    s8 = D // LANES
    R = xs.shape[0] // s8
    tf = _tile(F, tf)
    nf = F // tf

    def fstep(i, j, nu):
        return jnp.where(i < nu[0], j, nf - 1)

    return pl.pallas_call(
        _expert_kernel,
        grid_spec=pltpu.PrefetchScalarGridSpec(
            num_scalar_prefetch=2,
            grid=(R // tm, nf),
            in_specs=[
                pl.BlockSpec((tm * s8, LANES), lambda i, j, be, nu: (i, 0)),
                pl.BlockSpec((None, D, tf), lambda i, j, be, nu: (be[i], 0, fstep(i, j, nu))),
                pl.BlockSpec((None, D, tf), lambda i, j, be, nu: (be[i], 0, fstep(i, j, nu))),
                pl.BlockSpec((None, tf, D), lambda i, j, be, nu: (be[i], fstep(i, j, nu), 0)),
            ],
            out_specs=pl.BlockSpec((tm * s8, LANES), lambda i, j, be, nu: (i, 0)),
            scratch_shapes=[pltpu.VMEM((tm, D), BF16), pltpu.VMEM((tm, D), F32)],
        ),
        out_shape=jax.ShapeDtypeStruct((R * s8, LANES), F32),
        compiler_params=_params("parallel", "arbitrary"),
        name="expert_ffn",
    )(block_expert, n_used, xs, wg, wu, wd)


def _combine_kernel(dest_ref, ys_hbm, h_ref, gate_ref, g_ref, o_ref, buf_ref, sem, *, final_norm):
    tq, d = h_ref.shape
    s8 = d // LANES

    def issue(t, carry):
        for k in range(TOP_K):
            _token_copy(ys_hbm, buf_ref, sem, dest_ref[TOP_K * t + k], k * tq + t, s8).start()
        return carry

    lax.fori_loop(0, tq, issue, 0)
    pltpu.make_async_copy(ys_hbm.at[pl.ds(0, TOP_K * tq * s8), :], buf_ref, sem).wait()
    gate = gate_ref[...]
    ff = _load_token_tiles(buf_ref, 0, tq, s8) * gate[:, 0:1]
    for k in range(1, TOP_K):
        ff = ff + _load_token_tiles(buf_ref, k * tq, tq, s8) * gate[:, k:k + 1]
    out = h_ref[...] + ff
    if final_norm:
        out = _rms(out, g_ref[...])
    o_ref[...] = out


def combine(h, ys, dest, gates, final_gain, *, tq=512):
    T, D = h.shape
    s8 = D // LANES
    tq = _tile(T, tq)
    final_norm = final_gain is not None
    gain = final_gain if final_norm else jnp.ones((D,), F32)
    return pl.pallas_call(
        functools.partial(_combine_kernel, final_norm=final_norm),
        grid=(T // tq,),
        in_specs=[
            pl.BlockSpec((TOP_K * tq,), lambda i: (i,), memory_space=pltpu.SMEM),
            pl.BlockSpec(memory_space=pl.ANY),
            pl.BlockSpec((tq, D), lambda i: (i, 0)),
            pl.BlockSpec((tq, TOP_K), lambda i: (i, 0)),
            pl.BlockSpec((1, D), lambda i: (0, 0)),
        ],
        out_specs=pl.BlockSpec((tq, D), lambda i: (i, 0)),
        out_shape=jax.ShapeDtypeStruct((T, D), F32),
        scratch_shapes=[pltpu.VMEM((TOP_K * tq * s8, LANES), F32), pltpu.SemaphoreType.DMA],
        compiler_params=_params("arbitrary"),
        name="combine",
    )(dest, ys, h, gates, gain.reshape(1, D))


def _final_norm_kernel(x_ref, g_ref, o_ref):
    o_ref[...] = _rms(x_ref[...], g_ref[...])


def final_norm(h, g, *, tm=1024):
    M, D = h.shape
    tm = _tile(M, tm)
    return pl.pallas_call(
        _final_norm_kernel,
        grid=(M // tm,),
        in_specs=[pl.BlockSpec((tm, D), lambda i: (i, 0)), pl.BlockSpec((1, D), lambda i: (0, 0))],
        out_specs=pl.BlockSpec((tm, D), lambda i: (i, 0)),
        out_shape=jax.ShapeDtypeStruct((M, D), F32),
        compiler_params=_params("parallel"),
        name="final_norm",
    )(h, g.reshape(1, D))


def _dft_cos_sin(n):
    j = lax.broadcasted_iota(I32, (n, n), 0)
    k = lax.broadcasted_iota(I32, (n, n), 1)
    ang = ((j * k) % n).astype(F32) * (2.0 * np.pi / n)
    return jnp.cos(ang), jnp.sin(ang)


def _route(idx, n_experts, tm):
    T = idx.shape[0]
    TK = T * TOP_K
    flat_e = idx.reshape(TK)
    onehot = (flat_e[:, None] == jnp.arange(n_experts, dtype=I32)[None, :]).astype(I32)
    csum = jnp.cumsum(onehot, axis=0)
    rank = jnp.sum(csum * onehot, axis=1) - 1
    counts = csum[-1]
    padded = (counts + tm - 1) // tm * tm
    pend = jnp.cumsum(padded)
    pstart = pend - padded
    dest = (pstart[flat_e] + rank).astype(I32)
    n_blocks = -(-(TK + n_experts * (tm - 1)) // tm)
    block_expert = jnp.minimum(
        jnp.searchsorted(pend, jnp.arange(n_blocks, dtype=I32) * tm, side="right"), n_experts - 1).astype(I32)
    n_used = (pend[-1:] // tm).astype(I32)
    return dest, block_expert, n_used, n_blocks * tm


def _moe_block_rows(T):
    return _tile(T * TOP_K, 512)


def kernel(x, norm_mix, norm_ffn, fourier_w_in, fourier_w_out, nat_w_qkv, nat_rpb, nat_w_o,
           ffn_w_gate, ffn_w_up, ffn_w_down, router_w, router_b,
           expert_w_gate, expert_w_up, expert_w_down, final_norm_g):
    B, S, D = x.shape
    T = B * S
    depth = norm_mix.shape[0]
    n_experts = router_w.shape[2]
    cg = D // N_FOURIER_GROUPS

    cc, sc = _dft_cos_sin(cg)
    eye = jnp.eye(N_FOURIER_GROUPS, dtype=F32)
    chan = jnp.concatenate([jnp.kron(eye, cc), jnp.kron(eye, sc)], axis=1)
    dft_scale = float(1.0 / np.sqrt(S * cg))
    r1, r2 = S // GRID_W, GRID_W

    h = x.reshape(T, D)
    for i in range(depth):
        j = i // 2
        last = i == depth - 1
        if i % 2 == 0:
            w_ab = matmul_f32(fourier_w_in[j], chan, out_dtype=BF16)
            q = fourier_in(h, norm_mix[i], w_ab, batch=B, r1=r1, r2=r2)
            h = fft_stage2_out(q, fourier_w_out[j].astype(BF16), h, batch=B, r1=r1, r2=r2, scale=dft_scale)
            h = ffn(h, norm_ffn[i], ffn_w_gate[j].astype(BF16), ffn_w_up[j].astype(BF16),
                    ffn_w_down[j].astype(BF16))
            if last:
                h = final_norm(h, final_norm_g)
        else:
            qkv = norm_matmul(h, norm_mix[i], nat_w_qkv[j].astype(BF16))
            att = attention(qkv, nat_rpb[j], batch=B)
            h = matmul_res(att, nat_w_o[j].astype(BF16), h)
            hn, idx, gates = router(h, norm_ffn[i], router_w[j], router_b[j])
            tm = _moe_block_rows(T)
            dest, block_expert, n_used, n_rows = _route(idx, n_experts, tm)
            xs = dispatch_rows(hn, dest, n_rows, s8=D // LANES)
            ys = expert_ffn(xs, block_expert, n_used, expert_w_gate[j].astype(BF16),
                            expert_w_up[j].astype(BF16), expert_w_down[j].astype(BF16), tm=tm)
            h = combine(h, ys, dest, gates, final_norm_g if last else None)
    return h.reshape(B, S, D)
```

```python
import functools

import numpy as np
import jax
import jax.numpy as jnp
from jax import lax
from jax.experimental import pallas as pl
from jax.experimental.pallas import tpu as pltpu

F32 = jnp.float32
BF16 = jnp.bfloat16
I32 = jnp.int32

GRID_W = 64
N_FOURIER_GROUPS = 4
TOP_K = 2
RMS_EPS = 1e-6
MASK_VALUE = -1e30

LANES = 128
SUBLANES = 8

V7X_VMEM_BYTES = 64 * 1024 * 1024
VMEM_LIMIT_BYTES = V7X_VMEM_BYTES - 8 * 1024 * 1024


def _tile(n, target):
    if n <= target:
        return n
    for t in range(target, 0, -1):
        if n % t == 0:
            return t
    return n


def _params(*sem):
    return pltpu.CompilerParams(dimension_semantics=sem, vmem_limit_bytes=VMEM_LIMIT_BYTES)


def _rms(x, g):
    ms = jnp.mean(x * x, axis=-1, keepdims=True)
    return x * lax.rsqrt(ms + RMS_EPS) * g


def _dot(a, b):
    return jnp.dot(a, b, preferred_element_type=F32)


def _attn_col_plan(kw):
    W = GRID_W
    n_cb = W // kw
    wk = min(W, 2 * kw)
    starts = np.clip(np.arange(n_cb) * kw - kw // 2, 0, W - wk)
    c = np.arange(W)
    col_start = np.clip(c - kw // 2, 0, W - kw)
    blk = c // kw
    assert (col_start >= starts[blk]).all() and (col_start + kw <= starts[blk] + wk).all()
    return n_cb, wk, [int(s) for s in starts]


def _qkv_kernel(x_ref, g_ref, w_ref, q_ref, k_ref, v_ref, *, col_starts, wk):
    tm, d = q_ref.shape
    y = _dot(_rms(x_ref[...], g_ref[...]).astype(BF16), w_ref[...])
    q_ref[...] = y[:, :d].astype(BF16)
    n_cb = len(col_starts)
    for r in range(tm // GRID_W):
        for j, cs in enumerate(col_starts):
            src = slice(r * GRID_W + cs, r * GRID_W + cs + wk)
            dst = slice((r * n_cb + j) * wk, (r * n_cb + j + 1) * wk)
            k_ref[dst, :] = y[src, d:2 * d].astype(BF16)
            v_ref[dst, :] = y[src, 2 * d:].astype(BF16)


def qkv_proj(h, g, w, *, kw, tm=512):
    M, K = h.shape
    D = w.shape[1] // 3
    n_cb, wk, col_starts = _attn_col_plan(kw)
    tm = _tile(M, tm)
    assert tm % GRID_W == 0
    tkv = tm // GRID_W * n_cb * wk
    kv_rows = M // GRID_W * n_cb * wk
    return pl.pallas_call(
        functools.partial(_qkv_kernel, col_starts=col_starts, wk=wk),
        grid=(M // tm,),
        in_specs=[
            pl.BlockSpec((tm, K), lambda i: (i, 0)),
            pl.BlockSpec((1, K), lambda i: (0, 0)),
            pl.BlockSpec((K, 3 * D), lambda i: (0, 0)),
        ],
        out_specs=[
            pl.BlockSpec((tm, D), lambda i: (i, 0)),
            pl.BlockSpec((tkv, D), lambda i: (i, 0)),
            pl.BlockSpec((tkv, D), lambda i: (i, 0)),
        ],
        out_shape=[
            jax.ShapeDtypeStruct((M, D), BF16),
            jax.ShapeDtypeStruct((kv_rows, D), BF16),
            jax.ShapeDtypeStruct((kv_rows, D), BF16),
        ],
        compiler_params=_params("parallel"),
        name="qkv_proj",
    )(h, g.reshape(1, K), w)


def _matmul_res_kernel(a_ref, w_ref, r_ref, o_ref):
    o_ref[...] = r_ref[...] + _dot(a_ref[...], w_ref[...])


def matmul_res(a, w, res, *, tm=1024, tn=1024):
    M, K = a.shape
    N = w.shape[1]
    tm, tn = _tile(M, tm), _tile(N, tn)
    return pl.pallas_call(
        _matmul_res_kernel,
        grid=(M // tm, N // tn),
        in_specs=[
            pl.BlockSpec((tm, K), lambda i, j: (i, 0)),
            pl.BlockSpec((K, tn), lambda i, j: (0, j)),
            pl.BlockSpec((tm, tn), lambda i, j: (i, j)),
        ],
        out_specs=pl.BlockSpec((tm, tn), lambda i, j: (i, j)),
        out_shape=jax.ShapeDtypeStruct((M, N), F32),
        compiler_params=_params("parallel", "parallel"),
        name="matmul_res",
    )(a, w, res)


def _matmul_f32_kernel(a_ref, b_ref, o_ref):
    o_ref[...] = jnp.dot(a_ref[...], b_ref[...], preferred_element_type=F32,
                         precision=lax.Precision.HIGHEST).astype(o_ref.dtype)


def matmul_f32(a, b, *, out_dtype, tm=512, tn=512):
    M, K = a.shape
    N = b.shape[1]
    tm, tn = _tile(M, tm), _tile(N, tn)
    return pl.pallas_call(
        _matmul_f32_kernel,
        grid=(M // tm, N // tn),
        in_specs=[
            pl.BlockSpec((tm, K), lambda i, j: (i, 0)),
            pl.BlockSpec((K, tn), lambda i, j: (0, j)),
        ],
        out_specs=pl.BlockSpec((tm, tn), lambda i, j: (i, j)),
        out_shape=jax.ShapeDtypeStruct((M, N), out_dtype),
        compiler_params=_params("parallel", "parallel"),
        name="matmul_f32",
    )(a, b)


def _dft_np(n):
    j = np.arange(n)
    ang = 2.0 * np.pi * ((j[:, None] * j[None, :]) % n) / n
    return np.cos(ang), np.sin(ang)


def _fourier_in_kernel(x_ref, g_ref, w_ref, ga_ref, gb_ref, twc_ref, tws_ref, o_ref):
    r1, n_c, dm = x_ref.shape
    n = r1 * n_c
    xn = _rms(x_ref[...].reshape(n, dm), g_ref[...]).astype(BF16)
    ab = _dot(xn, w_ref[...])
    d = ab.shape[1] // 2
    a, b = ab[:, :d].astype(BF16), ab[:, d:].astype(BF16)
    p = _dot(ga_ref[...], a) + _dot(gb_ref[...], b)
    pr, pi = p[:n], p[n:]
    tc, ts = twc_ref[...], tws_ref[...]
    q = jnp.concatenate([pr * tc + pi * ts, pi * tc - pr * ts], axis=1)
    o_ref[...] = q.reshape(r1, n_c, 2 * d)


def fourier_in(h, g, w_ab, *, batch, r1, r2, cols_per_step=SUBLANES):
    T, D = h.shape
    D2 = w_ab.shape[1]
    n_c = _tile(r2, cols_per_step)
    n = r1 * n_c
    c1, s1 = _dft_np(r1)
    eye = np.eye(n_c)
    ga = jnp.asarray(np.kron(np.concatenate([c1, -s1], axis=0), eye), BF16)
    gb = jnp.asarray(np.kron(np.concatenate([-s1, -c1], axis=0), eye), BF16)
    c = np.arange(r2).reshape(r2 // n_c, 1, n_c)
    kc = np.arange(r1).reshape(1, r1, 1)
    ang = (2.0 * np.pi * (c * kc) / (r1 * r2)).reshape(r2 // n_c, n, 1)
    twc, tws = jnp.asarray(np.cos(ang), F32), jnp.asarray(np.sin(ang), F32)
    out = pl.pallas_call(
        _fourier_in_kernel,
        grid=(batch, r2 // n_c),
        in_specs=[
            pl.BlockSpec((None, r1, n_c, D), lambda b, c: (b, 0, c, 0)),
            pl.BlockSpec((1, D), lambda b, c: (0, 0)),
            pl.BlockSpec((D, D2), lambda b, c: (0, 0)),
            pl.BlockSpec((2 * n, n), lambda b, c: (0, 0)),
            pl.BlockSpec((2 * n, n), lambda b, c: (0, 0)),
            pl.BlockSpec((None, n, 1), lambda b, c: (c, 0, 0)),
            pl.BlockSpec((None, n, 1), lambda b, c: (c, 0, 0)),
        ],
        out_specs=pl.BlockSpec((None, r1, n_c, D2), lambda b, c: (b, 0, c, 0)),
        out_shape=jax.ShapeDtypeStruct((batch, r1, r2, D2), F32),
        compiler_params=_params("parallel", "parallel"),
        name="fourier_in",
    )(h.reshape(batch, r1, r2, D), g.reshape(1, D), w_ab, ga, gb, twc, tws)
    return out.reshape(T, D2)


def _fft_stage2_kernel(c_ref, s_ref, q_ref, w_ref, r_ref, o_ref, f_ref, *, scale):
    kb, r2, d2 = q_ref.shape
    d = d2 // 2
    for j in range(kb):
        qr, qi = q_ref[j, :, :d].astype(BF16), q_ref[j, :, d:].astype(BF16)
        f = _dot(c_ref[...], qr) + _dot(s_ref[...], qi)
        f_ref[j * r2:(j + 1) * r2, :] = (f * scale).astype(BF16)
    g = _dot(f_ref[...], w_ref[...])
    for j in range(kb):
        o_ref[:, j, :] = r_ref[:, j, :] + g[j * r2:(j + 1) * r2, :]


def fft_stage2_out(q, w_out, res, *, batch, r1, r2, scale, kc_per_step=8):
    T, D2 = q.shape
    D = D2 // 2
    kb = _tile(r1, kc_per_step)
    c2, s2 = _dft_np(r2)
    q4 = q.reshape(batch, r1, r2, D2)
    res4 = res.reshape(batch, r2, r1, D)
    out = pl.pallas_call(
        functools.partial(_fft_stage2_kernel, scale=scale),
        grid=(batch, r1 // kb),
        in_specs=[
            pl.BlockSpec((r2, r2), lambda b, k: (0, 0)),
            pl.BlockSpec((r2, r2), lambda b, k: (0, 0)),
            pl.BlockSpec((None, kb, r2, D2), lambda b, k: (b, k, 0, 0)),
            pl.BlockSpec((D, D), lambda b, k: (0, 0)),
            pl.BlockSpec((None, r2, kb, D), lambda b, k: (b, 0, k, 0)),
        ],
        out_specs=pl.BlockSpec((None, r2, kb, D), lambda b, k: (b, 0, k, 0)),
        out_shape=jax.ShapeDtypeStruct((batch, r2, r1, D), F32),
        scratch_shapes=[pltpu.VMEM((kb * r2, D), BF16)],
        compiler_params=_params("parallel", "parallel"),
        name="fft_stage2_out",
    )(jnp.asarray(c2, BF16), jnp.asarray(s2, BF16), q4, w_out, res4)
    return out.reshape(T, D)


def _ffn_kernel(x_ref, g_ref, wg_ref, wu_ref, wd_ref, o_ref, xn_ref, acc_ref):
    j = pl.program_id(1)

    @pl.when(j == 0)
    def _():
        xn_ref[...] = _rms(x_ref[...], g_ref[...]).astype(BF16)
        acc_ref[...] = jnp.zeros_like(acc_ref)

    xn = xn_ref[...]
    a = (jax.nn.silu(_dot(xn, wg_ref[...])) * _dot(xn, wu_ref[...])).astype(BF16)
    acc_ref[...] += _dot(a, wd_ref[...])

    @pl.when(j == pl.num_programs(1) - 1)
    def _():
        o_ref[...] = x_ref[...] + acc_ref[...]


def ffn(h, g, wg, wu, wd, *, tm=512, tf=2816):
    M, D = h.shape
    F = wg.shape[1]
    tm, tf = _tile(M, tm), _tile(F, tf)
    return pl.pallas_call(
        _ffn_kernel,
        grid=(M // tm, F // tf),
        in_specs=[
            pl.BlockSpec((tm, D), lambda i, j: (i, 0)),
            pl.BlockSpec((1, D), lambda i, j: (0, 0)),
            pl.BlockSpec((D, tf), lambda i, j: (0, j)),
            pl.BlockSpec((D, tf), lambda i, j: (0, j)),
            pl.BlockSpec((tf, D), lambda i, j: (j, 0)),
        ],
        out_specs=pl.BlockSpec((tm, D), lambda i, j: (i, 0)),
        out_shape=jax.ShapeDtypeStruct((M, D), F32),
        scratch_shapes=[pltpu.VMEM((tm, D), BF16), pltpu.VMEM((tm, D), F32)],
        compiler_params=_params("parallel", "arbitrary"),
        name="ffn",
    )(h, g.reshape(1, D), wg, wu, wd)


def _attn_kernel(ks_ref, tid_ref, dr_ref, q_ref, k_ref, v_ref, tab_ref, o_ref, b_ref, *, n_heads, scale):
    R, C, D = q_ref.shape
    KR, WK = k_ref.shape[1], k_ref.shape[3]
    hd = D // n_heads
    nq = R * C
    nk = KR * WK

    i, bidx = pl.program_id(1), pl.program_id(2)
    tid = tid_ref[i]
    changed = (i == 0) | (tid != tid_ref[jnp.maximum(i - 1, 0)])

    @pl.when((bidx == 0) & changed)
    def _():
        def per_head(h, carry):
            for rq in range(R):
                for rk in range(KR):
                    dr = dr_ref[(tid * R + rq) * KR + rk]
                    b_ref[h, rq * C:(rq + 1) * C, rk * WK:(rk + 1) * WK] = tab_ref[h, dr]
            return carry

        lax.fori_loop(0, n_heads, per_head, 0)

    lanes = min(D, max(hd, LANES))
    per_group = lanes // hd
    fold_scale = float(np.frexp(scale)[0]) == 0.5
    lane = lax.broadcasted_iota(I32, (1, lanes), 1)
    n_groups = D // lanes
    masks = [(lane >= hh * hd) & (lane < (hh + 1) * hd) for hh in range(per_group)]

    logits = []
    for g in range(n_groups):
        sl = slice(g * lanes, (g + 1) * lanes)
        q = q_ref[:, :, sl].reshape(nq, lanes)
        k = k_ref[0, :, 0, :, sl].reshape(nk, lanes)
        if fold_scale:
            q = q * scale
        for hh in range(per_group):
            qh = jnp.where(masks[hh], q, jnp.zeros_like(q)) if per_group > 1 else q
            s = lax.dot_general(qh, k, (((1,), (1,)), ((), ())), preferred_element_type=F32)
            if not fold_scale:
                s = s * scale
            logits.append(s + b_ref[g * per_group + hh])
    probs, inv_ls = [], []
    for s in logits:
        p = jnp.exp(s - jnp.max(s, axis=-1, keepdims=True))
        inv_ls.append(1.0 / jnp.sum(p, axis=-1, keepdims=True))
        probs.append(p.astype(BF16))
    for g in range(n_groups):
        sl = slice(g * lanes, (g + 1) * lanes)
        v = v_ref[0, :, 0, :, sl].reshape(nk, lanes)
        out = None
        for hh in range(per_group):
            h = g * per_group + hh
            o = _dot(probs[h], v) * inv_ls[h]
            out = o if out is None else jnp.where(masks[hh], o, out)
        o_ref[:, :, sl] = out.reshape(R, C, lanes).astype(o_ref.dtype)


def _attn_plan(rows, kh, R):
    KR = min(rows, R + kh)
    r = np.arange(rows)
    row_start = np.clip(r - kh // 2, 0, rows - kh)
    nblk = rows // R
    ks = np.clip(np.arange(nblk) * R - kh // 2, 0, rows - KR)
    sigs, tid = [], []
    for i in range(nblk):
        rs = row_start[i * R:(i + 1) * R]
        assert (rs >= ks[i]).all() and (rs + kh <= ks[i] + KR).all()
        sig = (i * R - ks[i],) + tuple(rs - ks[i])
        if sig not in sigs:
            sigs.append(sig)
        tid.append(sigs.index(sig))
    return KR, ks.astype(np.int32), np.asarray(tid, np.int32), sigs


def _attn_bias_table(rpb, kh, kw, R, KR, sigs, wk, col_starts):
    H = rpb.shape[0]
    win_h_max = (rpb.shape[1] + 1) // 2
    W = GRID_W
    c = np.arange(W)
    col_start = np.clip(c - kw // 2, 0, W - kw)
    kc = np.arange(W)
    col_ok = (kc[None, :] >= col_start[:, None]) & (kc[None, :] < col_start[:, None] + kw)
    dc = kc[None, :] - c[:, None] + (kw - 1)
    onehot = (dc[None] == np.arange(2 * kw - 1)[:, None, None]) & col_ok[None]
    tab = jnp.einsum("hrd,dck->hrck", rpb.astype(F32), jnp.asarray(onehot, F32),
                     precision=lax.Precision.HIGHEST)
    tab = jnp.where(jnp.asarray(col_ok), tab, MASK_VALUE)
    n_dr = 2 * win_h_max - 1
    tab = jnp.concatenate([tab, jnp.full((H, 1, W, W), MASK_VALUE, F32)], axis=1)
    tab = jnp.stack([tab[:, :, j * kw:(j + 1) * kw, cs:cs + wk] for j, cs in enumerate(col_starts)], axis=0)
    dr_of =np.full((len(sigs), R, KR), n_dr, np.int32)
    for t, sig in enumerate(sigs):
        q0, rs = sig[0], sig[1:]
        for rq in range(R):
            for rk in range(KR):
                if rs[rq] <= rk < rs[rq] + kh:
                    dr_of[t, rq, rk] = rk - (q0 + rq) + (win_h_max - 1)
    assert dr_of.min() >= 0
    return tab, dr_of.reshape(-1)


def attention(q, k, v, rpb, *, batch, rows_per_step=8):
    T, D = q.shape
    S = T // batch
    W = GRID_W
    rows = S // W
    H = rpb.shape[0]
    kh = min((rpb.shape[1] + 1) // 2, rows)
    kw = (rpb.shape[2] + 1) // 2
    hd = D // H
    n_cb, wk, col_starts = _attn_col_plan(kw)
    R = _tile(rows, rows_per_step)
    KR, ks, tid, sigs = _attn_plan(rows, kh, R)
    tab, dr_of = _attn_bias_table(rpb, kh, kw, R, KR, sigs, wk, col_starts)
    q5 = q.reshape(batch, rows, n_cb, kw, D)
    k5 = k.reshape(batch, rows, n_cb, wk, D)
    v5 = v.reshape(batch, rows, n_cb, wk, D)

    kv_spec = pl.BlockSpec(
        (pl.Element(1), pl.Element(KR), pl.Element(1), pl.Element(wk), pl.Element(D)),
        lambda j, i, b, ks_ref, tid_ref, dr_ref: (b, ks_ref[i], j, 0, 0))
    q_spec = pl.BlockSpec((None, R, None, kw, D), lambda j, i, b, ks_ref, tid_ref, dr_ref: (b, i, j, 0, 0))

    out = pl.pallas_call(
        functools.partial(_attn_kernel, n_heads=H, scale=hd ** -0.5),
        grid_spec=pltpu.PrefetchScalarGridSpec(
            num_scalar_prefetch=3,
            grid=(n_cb, rows // R, batch),
            in_specs=[
                q_spec,
                kv_spec,
                kv_spec,
                pl.BlockSpec((None,) + tab.shape[1:], lambda j, i, b, ks_ref, tid_ref, dr_ref: (j, 0, 0, 0, 0)),
            ],
            out_specs=q_spec,
            scratch_shapes=[pltpu.VMEM((H, R * kw, KR * wk), F32)],
        ),
        out_shape=jax.ShapeDtypeStruct((batch, rows, n_cb, kw, D), BF16),
        compiler_params=_params("arbitrary", "arbitrary", "arbitrary"),
        name="attention",
    )(jnp.asarray(ks), jnp.asarray(tid), jnp.asarray(dr_of), q5, k5, v5, tab)
    return out.reshape(T, D)


def _store_token_tiles(dst_ref, x):
    n, d = x.shape
    s8 = d // LANES
    for s in range(s8):
        dst_ref[pl.ds(s, n, stride=s8), :] = x[:, s * LANES:(s + 1) * LANES]


def _load_token_tiles(src_ref, first, n, s8):
    return jnp.concatenate(
        [src_ref[pl.ds(first * s8 + s, n, stride=s8), :] for s in range(s8)], axis=1)


def _token_copy(src_ref, dst_ref, sem, src_tok, dst_tok, s8):
    return pltpu.make_async_copy(
        src_ref.at[pl.ds(pl.multiple_of(src_tok * s8, s8), s8), :],
        dst_ref.at[pl.ds(pl.multiple_of(dst_tok * s8, s8), s8), :], sem)


def _check_token_tiles(d):
    assert d % (SUBLANES * LANES) == 0, "token-tile layout needs whole (8, 128) tiles per token"
    return d // LANES


def _router_kernel(x_ref, g_ref, wr_ref, br_ref, hn_ref, idx_ref, gate_ref):
    hn = _rms(x_ref[...], g_ref[...])
    _store_token_tiles(hn_ref, hn)
    logits = jnp.dot(hn, wr_ref[...], preferred_element_type=F32,
                     precision=lax.Precision.HIGHEST) + br_ref[...]
    n_e = logits.shape[1]
    lane = lax.broadcasted_iota(I32, logits.shape, 1)
    v1 = jnp.max(logits, axis=-1, keepdims=True)
    i1 = jnp.min(jnp.where(logits == v1, lane, n_e), axis=-1, keepdims=True)
    rest = jnp.where(lane == i1, -jnp.inf, logits)
    v2 = jnp.max(rest, axis=-1, keepdims=True)
    i2 = jnp.min(jnp.where(rest == v2, lane, n_e), axis=-1, keepdims=True)
    e2 = jnp.exp(v2 - v1)
    first = lax.broadcasted_iota(I32, idx_ref.shape, 1) == 0
    idx_ref[...] = jnp.where(first, i1, i2)
    gate_ref[...] = jnp.where(first, 1.0, e2) / (1.0 + e2)


def router(h, g, wr, br, *, tm=1024):
    M, D = h.shape
    E = wr.shape[1]
    s8 = _check_token_tiles(D)
    tm = _tile(M, tm)
    return pl.pallas_call(
        _router_kernel,
        grid=(M // tm,),
        in_specs=[
            pl.BlockSpec((tm, D), lambda i: (i, 0)),
            pl.BlockSpec((1, D), lambda i: (0, 0)),
            pl.BlockSpec((D, E), lambda i: (0, 0)),
            pl.BlockSpec((1, E), lambda i: (0, 0)),
        ],
        out_specs=[
            pl.BlockSpec((tm * s8, LANES), lambda i: (i, 0)),
            pl.BlockSpec((tm, TOP_K), lambda i: (i, 0)),
            pl.BlockSpec((tm, TOP_K), lambda i: (i, 0)),
        ],
        out_shape=[
            jax.ShapeDtypeStruct((M * s8, LANES), F32),
            jax.ShapeDtypeStruct((M, TOP_K), I32),
            jax.ShapeDtypeStruct((M, TOP_K), F32),
        ],
        compiler_params=_params("parallel"),
        name="router",
    )(h, g.reshape(1, D), wr, br.reshape(1, E))


def _dispatch_kernel(dest_ref, hn_ref, xs_in_hbm, xs_hbm, sem, *, s8):
    del xs_in_hbm
    tq = hn_ref.shape[0] // s8

    def issue(t, carry):
        for k in range(TOP_K):
            _token_copy(hn_ref, xs_hbm, sem, t, dest_ref[TOP_K * t + k], s8).start()
        return carry

    lax.fori_loop(0, tq, issue, 0)
    for _ in range(TOP_K):
        pltpu.make_async_copy(hn_ref, xs_hbm.at[pl.ds(0, tq * s8), :], sem).wait()


def dispatch_rows(hn, dest, n_rows, *, s8, tq=512):
    T = hn.shape[0] // s8
    tq = _tile(T, tq)
    return pl.pallas_call(
        functools.partial(_dispatch_kernel, s8=s8),
        grid=(T // tq,),
        in_specs=[
            pl.BlockSpec((TOP_K * tq,), lambda i: (i,), memory_space=pltpu.SMEM),
            pl.BlockSpec((tq * s8, LANES), lambda i: (i, 0)),
            pl.BlockSpec(memory_space=pl.ANY),
        ],
        out_specs=pl.BlockSpec(memory_space=pl.ANY),
        out_shape=jax.ShapeDtypeStruct((n_rows * s8, LANES), hn.dtype),
        scratch_shapes=[pltpu.SemaphoreType.DMA],
        input_output_aliases={2: 0},
        compiler_params=_params("arbitrary"),
        name="dispatch_rows",
    )(dest, hn, jnp.zeros((n_rows * s8, LANES), hn.dtype))


def _expert_kernel(be_ref, nu_ref, x_ref, wg_ref, wu_ref, wd_ref, o_ref, xb_ref, acc_ref):
    i, j = pl.program_id(0), pl.program_id(1)
    last = pl.num_programs(1) - 1
    used = i < nu_ref[0]
    tm, d = xb_ref.shape

    @pl.when(used)
    def _():
        @pl.when(j == 0)
        def _():
            xb_ref[...] = _load_token_tiles(x_ref, 0, tm, d // LANES).astype(BF16)
            acc_ref[...] = jnp.zeros_like(acc_ref)

        x = xb_ref[...]
        a = (jax.nn.silu(_dot(x, wg_ref[...])) * _dot(x, wu_ref[...])).astype(BF16)
        acc_ref[...] += _dot(a, wd_ref[...])

        @pl.when(j == last)
        def _():
            _store_token_tiles(o_ref, acc_ref[...])

    @pl.when(jnp.logical_not(used) & (j == last))
    def _():
        o_ref[...] = jnp.zeros_like(o_ref)


def expert_ffn(xs, block_expert, n_used, wg, wu, wd, *, layer, tm, tf=3584):
    D, F = wg.shape[2], wg.shape[3]
    s8 = D // LANES
    R = xs.shape[0] // s8
    tf = _tile(F, tf)
    nf = F // tf

    def fstep(i, j, nu):
        return jnp.where(i < nu[0], j, nf - 1)

    return pl.pallas_call(
        _expert_kernel,
        grid_spec=pltpu.PrefetchScalarGridSpec(
            num_scalar_prefetch=2,
            grid=(R // tm, nf),
            in_specs=[
                pl.BlockSpec((tm * s8, LANES), lambda i, j, be, nu: (i, 0)),
                pl.BlockSpec((None, None, D, tf), lambda i, j, be, nu: (layer, be[i], 0, fstep(i, j, nu))),
                pl.BlockSpec((None, None, D, tf), lambda i, j, be, nu: (layer, be[i], 0, fstep(i, j, nu))),
                pl.BlockSpec((None, None, tf, D), lambda i, j, be, nu: (layer, be[i], fstep(i, j, nu), 0)),
            ],
            out_specs=pl.BlockSpec((tm * s8, LANES), lambda i, j, be, nu: (i, 0)),
            scratch_shapes=[pltpu.VMEM((tm, D), BF16), pltpu.VMEM((tm, D), F32)],
        ),
        out_shape=jax.ShapeDtypeStruct((R * s8, LANES), F32),
        compiler_params=_params("parallel", "arbitrary"),
        name="expert_ffn",
    )(block_expert, n_used, xs, wg, wu, wd)


def _combine_kernel(dest_ref, ys_hbm, h_ref, gate_ref, g_ref, o_ref, buf_ref, sem, *, final_norm):
    tq, d = h_ref.shape
    s8 = d // LANES

    def issue(t, carry):
        for k in range(TOP_K):
            _token_copy(ys_hbm, buf_ref, sem, dest_ref[TOP_K * t + k], k * tq + t, s8).start()
        return carry

    lax.fori_loop(0, tq, issue, 0)
    pltpu.make_async_copy(ys_hbm.at[pl.ds(0, TOP_K * tq * s8), :], buf_ref, sem).wait()
    gate = gate_ref[...]
    ff = _load_token_tiles(buf_ref, 0, tq, s8) * gate[:, 0:1]
    for k in range(1, TOP_K):
        ff = ff + _load_token_tiles(buf_ref, k * tq, tq, s8) * gate[:, k:k + 1]
    out = h_ref[...] + ff
    if final_norm:
        out = _rms(out, g_ref[...])
    o_ref[...] = out


def combine(h, ys, dest, gates, final_gain, *, tq=512):
    T, D = h.shape
    s8 = D // LANES
    tq = _tile(T, tq)
    final_norm = final_gain is not None
    gain = final_gain if final_norm else jnp.ones((D,), F32)
    return pl.pallas_call(
        functools.partial(_combine_kernel, final_norm=final_norm),
        grid=(T // tq,),
        in_specs=[
            pl.BlockSpec((TOP_K * tq,), lambda i: (i,), memory_space=pltpu.SMEM),
            pl.BlockSpec(memory_space=pl.ANY),
            pl.BlockSpec((tq, D), lambda i: (i, 0)),
            pl.BlockSpec((tq, TOP_K), lambda i: (i, 0)),
            pl.BlockSpec((1, D), lambda i: (0, 0)),
        ],
        out_specs=pl.BlockSpec((tq, D), lambda i: (i, 0)),
        out_shape=jax.ShapeDtypeStruct((T, D), F32),
        scratch_shapes=[pltpu.VMEM((TOP_K * tq * s8, LANES), F32), pltpu.SemaphoreType.DMA],
        compiler_params=_params("arbitrary"),
        name="combine",
    )(dest, ys, h, gates, gain.reshape(1, D))


def _final_norm_kernel(x_ref, g_ref, o_ref):
    o_ref[...] = _rms(x_ref[...], g_ref[...])


def final_norm(h, g, *, tm=1024):
    M, D = h.shape
    tm = _tile(M, tm)
    return pl.pallas_call(
        _final_norm_kernel,
        grid=(M // tm,),
        in_specs=[pl.BlockSpec((tm, D), lambda i: (i, 0)), pl.BlockSpec((1, D), lambda i: (0, 0))],
        out_specs=pl.BlockSpec((tm, D), lambda i: (i, 0)),
        out_shape=jax.ShapeDtypeStruct((M, D), F32),
        compiler_params=_params("parallel"),
        name="final_norm",
    )(h, g.reshape(1, D))


def _dft_cos_sin(n):
    j = lax.broadcasted_iota(I32, (n, n), 0)
    k = lax.broadcasted_iota(I32, (n, n), 1)
    ang = ((j * k) % n).astype(F32) * (2.0 * np.pi / n)
    return jnp.cos(ang), jnp.sin(ang)


def _route(idx, n_experts, tm):
    T = idx.shape[0]
    TK = T * TOP_K
    flat_e = idx.reshape(TK)
    onehot = (flat_e[:, None] == jnp.arange(n_experts, dtype=I32)[None, :]).astype(I32)
    csum = jnp.cumsum(onehot, axis=0)
    rank = jnp.sum(csum * onehot, axis=1) - 1
    counts = csum[-1]
    padded = (counts + tm - 1) // tm * tm
    pend = jnp.cumsum(padded)
    pstart = pend - padded
    dest = (pstart[flat_e] + rank).astype(I32)
    n_blocks = -(-(TK + n_experts * (tm - 1)) // tm)
    block_expert = jnp.minimum(
        jnp.searchsorted(pend, jnp.arange(n_blocks, dtype=I32) * tm, side="right"), n_experts - 1).astype(I32)
    n_used = (pend[-1:] // tm).astype(I32)
    return dest, block_expert, n_used, n_blocks * tm


def _moe_block_rows(T):
    return _tile(T * TOP_K, 256)


def kernel(x, norm_mix, norm_ffn, fourier_w_in, fourier_w_out, nat_w_qkv, nat_rpb, nat_w_o,
           ffn_w_gate, ffn_w_up, ffn_w_down, router_w, router_b,
           expert_w_gate, expert_w_up, expert_w_down, final_norm_g):
    B, S, D = x.shape
    T = B * S
    depth = norm_mix.shape[0]
    n_experts = router_w.shape[2]
    cg = D // N_FOURIER_GROUPS

    cc, sc = _dft_cos_sin(cg)
    eye = jnp.eye(N_FOURIER_GROUPS, dtype=F32)
    chan = jnp.concatenate([jnp.kron(eye, cc), jnp.kron(eye, sc)], axis=1)
    dft_scale = float(1.0 / np.sqrt(S * cg))
    r1, r2 = S // GRID_W, GRID_W
    ewg, ewu, ewd = (w.astype(BF16) for w in (expert_w_gate, expert_w_up, expert_w_down))

    h = x.reshape(T, D)
    for i in range(depth):
        j = i // 2
        last = i == depth - 1
        if i % 2 == 0:
            w_ab = matmul_f32(fourier_w_in[j], chan, out_dtype=BF16)
            q = fourier_in(h, norm_mix[i], w_ab, batch=B, r1=r1, r2=r2)
            h = fft_stage2_out(q, fourier_w_out[j].astype(BF16), h, batch=B, r1=r1, r2=r2, scale=dft_scale)
            h = ffn(h, norm_ffn[i], ffn_w_gate[j].astype(BF16), ffn_w_up[j].astype(BF16),
                    ffn_w_down[j].astype(BF16))
            if last:
                h = final_norm(h, final_norm_g)
        else:
            kw = (nat_rpb.shape[3] + 1) // 2
            q, k, v = qkv_proj(h, norm_mix[i], nat_w_qkv[j].astype(BF16), kw=kw)
            att = attention(q, k, v, nat_rpb[j], batch=B)
            h = matmul_res(att, nat_w_o[j].astype(BF16), h)
            hn, idx, gates = router(h, norm_ffn[i], router_w[j], router_b[j])
            tm = _moe_block_rows(T)
            dest, block_expert, n_used, n_rows = _route(idx, n_experts, tm)
            xs = dispatch_rows(hn, dest, n_rows, s8=D // LANES)
            ys = expert_ffn(xs, block_expert, n_used, ewg, ewu, ewd, layer=j, tm=tm)
            h = combine(h, ys, dest, gates, final_norm_g if last else None)
    return h.reshape(B, S, D)
```

```python
import functools

import numpy as np
import jax
import jax.numpy as jnp
from jax import lax
from jax.experimental import pallas as pl
from jax.experimental.pallas import tpu as pltpu

F32 = jnp.float32
BF16 = jnp.bfloat16
I32 = jnp.int32

GRID_W = 64
N_FOURIER_GROUPS = 4
TOP_K = 2
RMS_EPS = 1e-6
MASK_VALUE = -1e30

LANES = 128
SUBLANES = 8

V7X_VMEM_BYTES = 64 * 1024 * 1024
VMEM_LIMIT_BYTES = V7X_VMEM_BYTES - 8 * 1024 * 1024


def _tile(n, target):
    if n <= target:
        return n
    for t in range(target, 0, -1):
        if n % t == 0:
            return t
    return n


def _params(*sem):
    return pltpu.CompilerParams(dimension_semantics=sem, vmem_limit_bytes=VMEM_LIMIT_BYTES)


def _rms(x, g):
    ms = jnp.mean(x * x, axis=-1, keepdims=True)
    return x * lax.rsqrt(ms + RMS_EPS) * g


def _dot(a, b):
    return jnp.dot(a, b, preferred_element_type=F32)


def _attn_col_plan(kw):
    W = GRID_W
    n_cb = W // kw
    wk = min(W, 2 * kw)
    starts = np.clip(np.arange(n_cb) * kw - kw // 2, 0, W - wk)
    c = np.arange(W)
    col_start = np.clip(c - kw // 2, 0, W - kw)
    blk = c // kw
    assert (col_start >= starts[blk]).all() and (col_start + kw <= starts[blk] + wk).all()
    return n_cb, wk, [int(s) for s in starts]


def _qkv_kernel(x_ref, g_ref, w_ref, q_ref, k_ref, v_ref, *, col_starts, wk):
    tm, d = q_ref.shape
    y = _dot(_rms(x_ref[...], g_ref[...]).astype(BF16), w_ref[...])
    q_ref[...] = y[:, :d].astype(BF16)
    n_cb = len(col_starts)
    for r in range(tm // GRID_W):
        for j, cs in enumerate(col_starts):
            src = slice(r * GRID_W + cs, r * GRID_W + cs + wk)
            dst = slice((r * n_cb + j) * wk, (r * n_cb + j + 1) * wk)
            k_ref[dst, :] = y[src, d:2 * d].astype(BF16)
            v_ref[dst, :] = y[src, 2 * d:].astype(BF16)


def qkv_proj(h, g, w, *, kw, tm=512):
    M, K = h.shape
    D = w.shape[1] // 3
    n_cb, wk, col_starts = _attn_col_plan(kw)
    tm = _tile(M, tm)
    assert tm % GRID_W == 0
    tkv = tm // GRID_W * n_cb * wk
    kv_rows = M // GRID_W * n_cb * wk
    return pl.pallas_call(
        functools.partial(_qkv_kernel, col_starts=col_starts, wk=wk),
        grid=(M // tm,),
        in_specs=[
            pl.BlockSpec((tm, K), lambda i: (i, 0)),
            pl.BlockSpec((1, K), lambda i: (0, 0)),
            pl.BlockSpec((K, 3 * D), lambda i: (0, 0)),
        ],
        out_specs=[
            pl.BlockSpec((tm, D), lambda i: (i, 0)),
            pl.BlockSpec((tkv, D), lambda i: (i, 0)),
            pl.BlockSpec((tkv, D), lambda i: (i, 0)),
        ],
        out_shape=[
            jax.ShapeDtypeStruct((M, D), BF16),
            jax.ShapeDtypeStruct((kv_rows, D), BF16),
            jax.ShapeDtypeStruct((kv_rows, D), BF16),
        ],
        compiler_params=_params("parallel"),
        name="qkv_proj",
    )(h, g.reshape(1, K), w)


def _matmul_res_kernel(a_ref, w_ref, r_ref, o_ref):
    o_ref[...] = r_ref[...] + _dot(a_ref[...], w_ref[...])


def matmul_res(a, w, res, *, tm=1024, tn=1024):
    M, K = a.shape
    N = w.shape[1]
    tm, tn = _tile(M, tm), _tile(N, tn)
    return pl.pallas_call(
        _matmul_res_kernel,
        grid=(M // tm, N // tn),
        in_specs=[
            pl.BlockSpec((tm, K), lambda i, j: (i, 0)),
            pl.BlockSpec((K, tn), lambda i, j: (0, j)),
            pl.BlockSpec((tm, tn), lambda i, j: (i, j)),
        ],
        out_specs=pl.BlockSpec((tm, tn), lambda i, j: (i, j)),
        out_shape=jax.ShapeDtypeStruct((M, N), F32),
        compiler_params=_params("parallel", "parallel"),
        name="matmul_res",
    )(a, w, res)


def _matmul_f32_kernel(a_ref, b_ref, o_ref):
    o_ref[...] = jnp.dot(a_ref[...], b_ref[...], preferred_element_type=F32,
                         precision=lax.Precision.HIGHEST).astype(o_ref.dtype)


def matmul_f32(a, b, *, out_dtype, tm=512, tn=512):
    M, K = a.shape
    N = b.shape[1]
    tm, tn = _tile(M, tm), _tile(N, tn)
    return pl.pallas_call(
        _matmul_f32_kernel,
        grid=(M // tm, N // tn),
        in_specs=[
            pl.BlockSpec((tm, K), lambda i, j: (i, 0)),
            pl.BlockSpec((K, tn), lambda i, j: (0, j)),
        ],
        out_specs=pl.BlockSpec((tm, tn), lambda i, j: (i, j)),
        out_shape=jax.ShapeDtypeStruct((M, N), out_dtype),
        compiler_params=_params("parallel", "parallel"),
        name="matmul_f32",
    )(a, b)


def _dft_np(n):
    j = np.arange(n)
    ang = 2.0 * np.pi * ((j[:, None] * j[None, :]) % n) / n
    return np.cos(ang), np.sin(ang)


def _fourier_in_kernel(x_ref, g_ref, w_ref, ga_ref, gb_ref, twc_ref, tws_ref, o_ref):
    r1, n_c, dm = x_ref.shape
    n = r1 * n_c
    xn = _rms(x_ref[...].reshape(n, dm), g_ref[...]).astype(BF16)
    ab = _dot(xn, w_ref[...])
    d = ab.shape[1] // 2
    a, b = ab[:, :d].astype(BF16), ab[:, d:].astype(BF16)
    p = _dot(ga_ref[...], a) + _dot(gb_ref[...], b)
    pr, pi = p[:n], p[n:]
    tc, ts = twc_ref[...], tws_ref[...]
    q = jnp.concatenate([pr * tc + pi * ts, pi * tc - pr * ts], axis=1)
    o_ref[...] = q.reshape(r1, n_c, 2 * d)


def fourier_in(h, g, w_ab, *, batch, r1, r2, cols_per_step=SUBLANES):
    T, D = h.shape
    D2 = w_ab.shape[1]
    n_c = _tile(r2, cols_per_step)
    n = r1 * n_c
    c1, s1 = _dft_np(r1)
    eye = np.eye(n_c)
    ga = jnp.asarray(np.kron(np.concatenate([c1, -s1], axis=0), eye), BF16)
    gb = jnp.asarray(np.kron(np.concatenate([-s1, -c1], axis=0), eye), BF16)
    c = np.arange(r2).reshape(r2 // n_c, 1, n_c)
    kc = np.arange(r1).reshape(1, r1, 1)
    ang = (2.0 * np.pi * (c * kc) / (r1 * r2)).reshape(r2 // n_c, n, 1)
    twc, tws = jnp.asarray(np.cos(ang), F32), jnp.asarray(np.sin(ang), F32)
    out = pl.pallas_call(
        _fourier_in_kernel,
        grid=(batch, r2 // n_c),
        in_specs=[
            pl.BlockSpec((None, r1, n_c, D), lambda b, c: (b, 0, c, 0)),
            pl.BlockSpec((1, D), lambda b, c: (0, 0)),
            pl.BlockSpec((D, D2), lambda b, c: (0, 0)),
            pl.BlockSpec((2 * n, n), lambda b, c: (0, 0)),
            pl.BlockSpec((2 * n, n), lambda b, c: (0, 0)),
            pl.BlockSpec((None, n, 1), lambda b, c: (c, 0, 0)),
            pl.BlockSpec((None, n, 1), lambda b, c: (c, 0, 0)),
        ],
        out_specs=pl.BlockSpec((None, r1, n_c, D2), lambda b, c: (b, 0, c, 0)),
        out_shape=jax.ShapeDtypeStruct((batch, r1, r2, D2), F32),
        compiler_params=_params("parallel", "parallel"),
        name="fourier_in",
    )(h.reshape(batch, r1, r2, D), g.reshape(1, D), w_ab, ga, gb, twc, tws)
    return out.reshape(T, D2)


def _fft_stage2_kernel(c_ref, s_ref, q_ref, w_ref, r_ref, o_ref, f_ref, *, scale):
    kb, r2, d2 = q_ref.shape
    d = d2 // 2
    for j in range(kb):
        qr, qi = q_ref[j, :, :d].astype(BF16), q_ref[j, :, d:].astype(BF16)
        f = _dot(c_ref[...], qr) + _dot(s_ref[...], qi)
        f_ref[j * r2:(j + 1) * r2, :] = (f * scale).astype(BF16)
    g = _dot(f_ref[...], w_ref[...])
    for j in range(kb):
        o_ref[:, j, :] = r_ref[:, j, :] + g[j * r2:(j + 1) * r2, :]


def fft_stage2_out(q, w_out, res, *, batch, r1, r2, scale, kc_per_step=8):
    T, D2 = q.shape
    D = D2 // 2
    kb = _tile(r1, kc_per_step)
    c2, s2 = _dft_np(r2)
    q4 = q.reshape(batch, r1, r2, D2)
    res4 = res.reshape(batch, r2, r1, D)
    out = pl.pallas_call(
        functools.partial(_fft_stage2_kernel, scale=scale),
        grid=(batch, r1 // kb),
        in_specs=[
            pl.BlockSpec((r2, r2), lambda b, k: (0, 0)),
            pl.BlockSpec((r2, r2), lambda b, k: (0, 0)),
            pl.BlockSpec((None, kb, r2, D2), lambda b, k: (b, k, 0, 0)),
            pl.BlockSpec((D, D), lambda b, k: (0, 0)),
            pl.BlockSpec((None, r2, kb, D), lambda b, k: (b, 0, k, 0)),
        ],
        out_specs=pl.BlockSpec((None, r2, kb, D), lambda b, k: (b, 0, k, 0)),
        out_shape=jax.ShapeDtypeStruct((batch, r2, r1, D), F32),
        scratch_shapes=[pltpu.VMEM((kb * r2, D), BF16)],
        compiler_params=_params("parallel", "parallel"),
        name="fft_stage2_out",
    )(jnp.asarray(c2, BF16), jnp.asarray(s2, BF16), q4, w_out, res4)
    return out.reshape(T, D)


def _ffn_kernel(x_ref, g_ref, wg_ref, wu_ref, wd_ref, o_ref):
    x = x_ref[...]
    xn = _rms(x, g_ref[...]).astype(BF16)
    a = (jax.nn.silu(_dot(xn, wg_ref[...])) * _dot(xn, wu_ref[...])).astype(BF16)
    o_ref[...] = x + _dot(a, wd_ref[...])


def ffn(h, g, wg, wu, wd, *, tm=512):
    M, D = h.shape
    F = wg.shape[1]
    tm = _tile(M, tm)
    return pl.pallas_call(
        _ffn_kernel,
        grid=(M // tm,),
        in_specs=[
            pl.BlockSpec((tm, D), lambda i: (i, 0)),
            pl.BlockSpec((1, D), lambda i: (0, 0)),
            pl.BlockSpec((D, F), lambda i: (0, 0)),
            pl.BlockSpec((D, F), lambda i: (0, 0)),
            pl.BlockSpec((F, D), lambda i: (0, 0)),
        ],
        out_specs=pl.BlockSpec((tm, D), lambda i: (i, 0)),
        out_shape=jax.ShapeDtypeStruct((M, D), F32),
        compiler_params=_params("parallel"),
        name="ffn",
    )(h, g.reshape(1, D), wg, wu, wd)


def _attn_kernel(ks_ref, tid_ref, dr_ref, q_ref, k_ref, v_ref, tab_ref, o_ref, b_ref, *, n_heads, scale):
    NB, R, C, D = q_ref.shape
    KR, WK = k_ref.shape[1], k_ref.shape[3]
    hd = D // n_heads
    nq = R * C
    nk = KR * WK

    i, bidx = pl.program_id(1), pl.program_id(2)
    tid = tid_ref[i]
    changed = (i == 0) | (tid != tid_ref[jnp.maximum(i - 1, 0)])

    @pl.when((bidx == 0) & changed)
    def _():
        def per_head(h, carry):
            for rq in range(R):
                for rk in range(KR):
                    dr = dr_ref[(tid * R + rq) * KR + rk]
                    b_ref[h, rq * C:(rq + 1) * C, rk * WK:(rk + 1) * WK] = tab_ref[h, dr]
            return carry

        lax.fori_loop(0, n_heads, per_head, 0)

    lanes = min(D, max(hd, LANES))
    per_group = lanes // hd
    fold_scale = float(np.frexp(scale)[0]) == 0.5
    lane = lax.broadcasted_iota(I32, (1, lanes), 1)
    n_groups = D // lanes
    masks = [(lane >= hh * hd) & (lane < (hh + 1) * hd) for hh in range(per_group)]

    logits = []
    for bb in range(NB):
        for g in range(n_groups):
            sl = slice(g * lanes, (g + 1) * lanes)
            q = q_ref[bb, :, :, sl].reshape(nq, lanes)
            k = k_ref[bb, :, 0, :, sl].reshape(nk, lanes)
            if fold_scale:
                q = q * scale
            for hh in range(per_group):
                qh = jnp.where(masks[hh], q, jnp.zeros_like(q)) if per_group > 1 else q
                s = lax.dot_general(qh, k, (((1,), (1,)), ((), ())), preferred_element_type=F32)
                if not fold_scale:
                    s = s * scale
                logits.append(s + b_ref[g * per_group + hh])
    probs, inv_ls = [], []
    for s in logits:
        p = jnp.exp(s - jnp.max(s, axis=-1, keepdims=True))
        inv_ls.append(1.0 / jnp.sum(p, axis=-1, keepdims=True))
        probs.append(p.astype(BF16))
    for bb in range(NB):
        for g in range(n_groups):
            sl = slice(g * lanes, (g + 1) * lanes)
            v = v_ref[bb, :, 0, :, sl].reshape(nk, lanes)
            out = None
            for hh in range(per_group):
                h = (bb * n_groups + g) * per_group + hh
                o = _dot(probs[h], v) * inv_ls[h]
                out = o if out is None else jnp.where(masks[hh], o, out)
            o_ref[bb, :, :, sl] = out.reshape(R, C, lanes).astype(o_ref.dtype)


def _attn_plan(rows, kh, R):
    KR = min(rows, R + kh)
    r = np.arange(rows)
    row_start = np.clip(r - kh // 2, 0, rows - kh)
    nblk = rows // R
    ks = np.clip(np.arange(nblk) * R - kh // 2, 0, rows - KR)
    sigs, tid = [], []
    for i in range(nblk):
        rs = row_start[i * R:(i + 1) * R]
        assert (rs >= ks[i]).all() and (rs + kh <= ks[i] + KR).all()
        sig = (i * R - ks[i],) + tuple(rs - ks[i])
        if sig not in sigs:
            sigs.append(sig)
        tid.append(sigs.index(sig))
    return KR, ks.astype(np.int32), np.asarray(tid, np.int32), sigs


def _attn_bias_table(rpb, kh, kw, R, KR, sigs, wk, col_starts):
    H = rpb.shape[0]
    win_h_max = (rpb.shape[1] + 1) // 2
    W = GRID_W
    c = np.arange(W)
    col_start = np.clip(c - kw // 2, 0, W - kw)
    kc = np.arange(W)
    col_ok = (kc[None, :] >= col_start[:, None]) & (kc[None, :] < col_start[:, None] + kw)
    dc = kc[None, :] - c[:, None] + (kw - 1)
    onehot = (dc[None] == np.arange(2 * kw - 1)[:, None, None]) & col_ok[None]
    tab = jnp.einsum("hrd,dck->hrck", rpb.astype(F32), jnp.asarray(onehot, F32),
                     precision=lax.Precision.HIGHEST)
    tab = jnp.where(jnp.asarray(col_ok), tab, MASK_VALUE)
    n_dr = 2 * win_h_max - 1
    tab = jnp.concatenate([tab, jnp.full((H, 1, W, W), MASK_VALUE, F32)], axis=1)
    tab = jnp.stack([tab[:, :, j * kw:(j + 1) * kw, cs:cs + wk] for j, cs in enumerate(col_starts)], axis=0)
    dr_of =np.full((len(sigs), R, KR), n_dr, np.int32)
    for t, sig in enumerate(sigs):
        q0, rs = sig[0], sig[1:]
        for rq in range(R):
            for rk in range(KR):
                if rs[rq] <= rk < rs[rq] + kh:
                    dr_of[t, rq, rk] = rk - (q0 + rq) + (win_h_max - 1)
    assert dr_of.min() >= 0
    return tab, dr_of.reshape(-1)


def attention(q, k, v, rpb, *, batch, rows_per_step=8, batch_per_step=4):
    T, D = q.shape
    nb = _tile(batch, batch_per_step)
    S = T // batch
    W = GRID_W
    rows = S // W
    H = rpb.shape[0]
    kh = min((rpb.shape[1] + 1) // 2, rows)
    kw = (rpb.shape[2] + 1) // 2
    hd = D // H
    n_cb, wk, col_starts = _attn_col_plan(kw)
    R = _tile(rows, rows_per_step)
    KR, ks, tid, sigs = _attn_plan(rows, kh, R)
    tab, dr_of = _attn_bias_table(rpb, kh, kw, R, KR, sigs, wk, col_starts)
    q5 = q.reshape(batch, rows, n_cb, kw, D)
    k5 = k.reshape(batch, rows, n_cb, wk, D)
    v5 = v.reshape(batch, rows, n_cb, wk, D)

    kv_spec = pl.BlockSpec(
        (pl.Element(nb), pl.Element(KR), pl.Element(1), pl.Element(wk), pl.Element(D)),
        lambda j, i, b, ks_ref, tid_ref, dr_ref: (b * nb, ks_ref[i], j, 0, 0))
    q_spec = pl.BlockSpec((nb, R, None, kw, D), lambda j, i, b, ks_ref, tid_ref, dr_ref: (b, i, j, 0, 0))

    out = pl.pallas_call(
        functools.partial(_attn_kernel, n_heads=H, scale=hd ** -0.5),
        grid_spec=pltpu.PrefetchScalarGridSpec(
            num_scalar_prefetch=3,
            grid=(n_cb, rows // R, batch // nb),
            in_specs=[
                q_spec,
                kv_spec,
                kv_spec,
                pl.BlockSpec((None,) + tab.shape[1:], lambda j, i, b, ks_ref, tid_ref, dr_ref: (j, 0, 0, 0, 0)),
            ],
            out_specs=q_spec,
            scratch_shapes=[pltpu.VMEM((H, R * kw, KR * wk), F32)],
        ),
        out_shape=jax.ShapeDtypeStruct((batch, rows, n_cb, kw, D), BF16),
        compiler_params=_params("arbitrary", "arbitrary", "arbitrary"),
        name="attention",
    )(jnp.asarray(ks), jnp.asarray(tid), jnp.asarray(dr_of), q5, k5, v5, tab)
    return out.reshape(T, D)


def _store_token_tiles(dst_ref, x):
    n, d = x.shape
    s8 = d // LANES
    for s in range(s8):
        dst_ref[pl.ds(s, n, stride=s8), :] = x[:, s * LANES:(s + 1) * LANES]


def _load_token_tiles(src_ref, first, n, s8):
    return jnp.concatenate(
        [src_ref[pl.ds(first * s8 + s, n, stride=s8), :] for s in range(s8)], axis=1)


def _token_copy(src_ref, dst_ref, sem, src_tok, dst_tok, s8):
    return pltpu.make_async_copy(
        src_ref.at[pl.ds(pl.multiple_of(src_tok * s8, s8), s8), :],
        dst_ref.at[pl.ds(pl.multiple_of(dst_tok * s8, s8), s8), :], sem)


def _check_token_tiles(d):
    assert d % (SUBLANES * LANES) == 0, "token-tile layout needs whole (8, 128) tiles per token"
    return d // LANES


def _router_kernel(x_ref, g_ref, wr_ref, br_ref, hn_ref, idx_ref, gate_ref):
    hn = _rms(x_ref[...], g_ref[...])
    _store_token_tiles(hn_ref, hn)
    w = wr_ref[...]
    hn_hi, w_hi = hn.astype(BF16), w.astype(BF16)
    hn_lo = (hn - hn_hi.astype(F32)).astype(BF16)
    w_lo = (w - w_hi.astype(F32)).astype(BF16)
    logits = _dot(jnp.concatenate([hn_hi, hn_lo, hn_hi], axis=1),
                  jnp.concatenate([w_hi, w_hi, w_lo], axis=0)) + br_ref[...]
    n_e = logits.shape[1]
    lane = lax.broadcasted_iota(I32, logits.shape, 1)
    v1 = jnp.max(logits, axis=-1, keepdims=True)
    i1 = jnp.min(jnp.where(logits == v1, lane, n_e), axis=-1, keepdims=True)
    rest = jnp.where(lane == i1, -jnp.inf, logits)
    v2 = jnp.max(rest, axis=-1, keepdims=True)
    i2 = jnp.min(jnp.where(rest == v2, lane, n_e), axis=-1, keepdims=True)
    e2 = jnp.exp(v2 - v1)
    first = lax.broadcasted_iota(I32, idx_ref.shape, 1) == 0
    idx_ref[...] = jnp.where(first, i1, i2)
    gate_ref[...] = jnp.where(first, 1.0, e2) / (1.0 + e2)


def router(h, g, wr, br, *, tm=1024):
    M, D = h.shape
    E = wr.shape[1]
    s8 = _check_token_tiles(D)
    tm = _tile(M, tm)
    return pl.pallas_call(
        _router_kernel,
        grid=(M // tm,),
        in_specs=[
            pl.BlockSpec((tm, D), lambda i: (i, 0)),
            pl.BlockSpec((1, D), lambda i: (0, 0)),
            pl.BlockSpec((D, E), lambda i: (0, 0)),
            pl.BlockSpec((1, E), lambda i: (0, 0)),
        ],
        out_specs=[
            pl.BlockSpec((tm * s8, LANES), lambda i: (i, 0)),
            pl.BlockSpec((tm, TOP_K), lambda i: (i, 0)),
            pl.BlockSpec((tm, TOP_K), lambda i: (i, 0)),
        ],
        out_shape=[
            jax.ShapeDtypeStruct((M * s8, LANES), F32),
            jax.ShapeDtypeStruct((M, TOP_K), I32),
            jax.ShapeDtypeStruct((M, TOP_K), F32),
        ],
        compiler_params=_params("parallel"),
        name="router",
    )(h, g.reshape(1, D), wr, br.reshape(1, E))


def _dispatch_kernel(dest_ref, hn_ref, xs_in_hbm, xs_hbm, sem, *, s8):
    del xs_in_hbm
    tq = hn_ref.shape[0] // s8

    def issue(t, carry):
        for k in range(TOP_K):
            _token_copy(hn_ref, xs_hbm, sem, t, dest_ref[TOP_K * t + k], s8).start()
        return carry

    lax.fori_loop(0, tq, issue, 0)
    for _ in range(TOP_K):
        pltpu.make_async_copy(hn_ref, xs_hbm.at[pl.ds(0, tq * s8), :], sem).wait()


def dispatch_rows(hn, dest, n_rows, *, s8, init=None, tq=512):
    if init is None:
        init = jnp.zeros((n_rows * s8, LANES), hn.dtype)
    T = hn.shape[0] // s8
    tq = _tile(T, tq)
    return pl.pallas_call(
        functools.partial(_dispatch_kernel, s8=s8),
        grid=(T // tq,),
        in_specs=[
            pl.BlockSpec((TOP_K * tq,), lambda i: (i,), memory_space=pltpu.SMEM),
            pl.BlockSpec((tq * s8, LANES), lambda i: (i, 0)),
            pl.BlockSpec(memory_space=pl.ANY),
        ],
        out_specs=pl.BlockSpec(memory_space=pl.ANY),
        out_shape=jax.ShapeDtypeStruct((n_rows * s8, LANES), hn.dtype),
        scratch_shapes=[pltpu.SemaphoreType.DMA],
        input_output_aliases={2: 0},
        compiler_params=_params("arbitrary"),
        name="dispatch_rows",
    )(dest, hn, init)


def _expert_kernel(be_ref, nu_ref, x_ref, wg_ref, wu_ref, wd_ref, o_ref):
    used = pl.program_id(0) < nu_ref[0]
    d = wg_ref.shape[0]
    s8 = d // LANES
    tm = x_ref.shape[0] // s8

    @pl.when(used)
    def _():
        x = _load_token_tiles(x_ref, 0, tm, s8).astype(BF16)
        a = (jax.nn.silu(_dot(x, wg_ref[...])) * _dot(x, wu_ref[...])).astype(BF16)
        _store_token_tiles(o_ref, _dot(a, wd_ref[...]))

    @pl.when(jnp.logical_not(used))
    def _():
        o_ref[...] = jnp.zeros_like(o_ref)


def expert_ffn(xs, block_expert, n_used, wg, wu, wd, *, layer, tm):
    D, F = wg.shape[2], wg.shape[3]
    s8 = D // LANES
    R = xs.shape[0] // s8
    return pl.pallas_call(
        _expert_kernel,
        grid_spec=pltpu.PrefetchScalarGridSpec(
            num_scalar_prefetch=2,
            grid=(R // tm,),
            in_specs=[
                pl.BlockSpec((tm * s8, LANES), lambda i, be, nu: (i, 0)),
                pl.BlockSpec((None, None, D, F), lambda i, be, nu: (layer, be[i], 0, 0)),
                pl.BlockSpec((None, None, D, F), lambda i, be, nu: (layer, be[i], 0, 0)),
                pl.BlockSpec((None, None, F, D), lambda i, be, nu: (layer, be[i], 0, 0)),
            ],
            out_specs=pl.BlockSpec((tm * s8, LANES), lambda i, be, nu: (i, 0)),
        ),
        out_shape=jax.ShapeDtypeStruct((R * s8, LANES), F32),
        compiler_params=_params("parallel"),
        name="expert_ffn",
    )(block_expert, n_used, xs, wg, wu, wd)


def _combine_kernel(dest_ref, ys_hbm, h_ref, gate_ref, g_ref, o_ref, buf_ref, sem, *, final_norm):
    tq, d = h_ref.shape
    s8 = d // LANES

    def issue(t, carry):
        for k in range(TOP_K):
            _token_copy(ys_hbm, buf_ref, sem, dest_ref[TOP_K * t + k], k * tq + t, s8).start()
        return carry

    lax.fori_loop(0, tq, issue, 0)
    pltpu.make_async_copy(ys_hbm.at[pl.ds(0, TOP_K * tq * s8), :], buf_ref, sem).wait()
    gate = gate_ref[...]
    ff = _load_token_tiles(buf_ref, 0, tq, s8) * gate[:, 0:1]
    for k in range(1, TOP_K):
        ff = ff + _load_token_tiles(buf_ref, k * tq, tq, s8) * gate[:, k:k + 1]
    out = h_ref[...] + ff
    if final_norm:
        out = _rms(out, g_ref[...])
    o_ref[...] = out


def combine(h, ys, dest, gates, final_gain, *, tq=512):
    T, D = h.shape
    s8 = D // LANES
    tq = _tile(T, tq)
    final_norm = final_gain is not None
    gain = final_gain if final_norm else jnp.ones((D,), F32)
    return pl.pallas_call(
        functools.partial(_combine_kernel, final_norm=final_norm),
        grid=(T // tq,),
        in_specs=[
            pl.BlockSpec((TOP_K * tq,), lambda i: (i,), memory_space=pltpu.SMEM),
            pl.BlockSpec(memory_space=pl.ANY),
            pl.BlockSpec((tq, D), lambda i: (i, 0)),
            pl.BlockSpec((tq, TOP_K), lambda i: (i, 0)),
            pl.BlockSpec((1, D), lambda i: (0, 0)),
        ],
        out_specs=pl.BlockSpec((tq, D), lambda i: (i, 0)),
        out_shape=jax.ShapeDtypeStruct((T, D), F32),
        scratch_shapes=[pltpu.VMEM((TOP_K * tq * s8, LANES), F32), pltpu.SemaphoreType.DMA],
        compiler_params=_params("arbitrary"),
        name="combine",
    )(dest, ys, h, gates, gain.reshape(1, D))


def _final_norm_kernel(x_ref, g_ref, o_ref):
    o_ref[...] = _rms(x_ref[...], g_ref[...])


def final_norm(h, g, *, tm=1024):
    M, D = h.shape
    tm = _tile(M, tm)
    return pl.pallas_call(
        _final_norm_kernel,
        grid=(M // tm,),
        in_specs=[pl.BlockSpec((tm, D), lambda i: (i, 0)), pl.BlockSpec((1, D), lambda i: (0, 0))],
        out_specs=pl.BlockSpec((tm, D), lambda i: (i, 0)),
        out_shape=jax.ShapeDtypeStruct((M, D), F32),
        compiler_params=_params("parallel"),
        name="final_norm",
    )(h, g.reshape(1, D))


def _dft_cos_sin(n):
    j = lax.broadcasted_iota(I32, (n, n), 0)
    k = lax.broadcasted_iota(I32, (n, n), 1)
    ang = ((j * k) % n).astype(F32) * (2.0 * np.pi / n)
    return jnp.cos(ang), jnp.sin(ang)


def _route(idx, n_experts, tm):
    T = idx.shape[0]
    TK = T * TOP_K
    flat_e = idx.reshape(TK)
    onehot = (flat_e[:, None] == jnp.arange(n_experts, dtype=I32)[None, :]).astype(I32)
    csum = jnp.cumsum(onehot, axis=0)
    rank = jnp.sum(csum * onehot, axis=1) - 1
    counts = csum[-1]
    padded = (counts + tm - 1) // tm * tm
    pend = jnp.cumsum(padded)
    pstart = pend - padded
    dest = (pstart[flat_e] + rank).astype(I32)
    n_blocks = -(-(TK + n_experts * (tm - 1)) // tm)
    block_expert = jnp.minimum(
        jnp.searchsorted(pend, jnp.arange(n_blocks, dtype=I32) * tm, side="right"), n_experts - 1).astype(I32)
    n_used = (pend[-1:] // tm).astype(I32)
    return dest, block_expert, n_used, n_blocks * tm


def _moe_block_rows(T):
    return _tile(T * TOP_K, 256)


def kernel(x, norm_mix, norm_ffn, fourier_w_in, fourier_w_out, nat_w_qkv, nat_rpb, nat_w_o,
           ffn_w_gate, ffn_w_up, ffn_w_down, router_w, router_b,
           expert_w_gate, expert_w_up, expert_w_down, final_norm_g):
    B, S, D = x.shape
    T = B * S
    depth = norm_mix.shape[0]
    n_experts = router_w.shape[2]
    cg = D // N_FOURIER_GROUPS

    cc, sc = _dft_cos_sin(cg)
    eye = jnp.eye(N_FOURIER_GROUPS, dtype=F32)
    chan = jnp.concatenate([jnp.kron(eye, cc), jnp.kron(eye, sc)], axis=1)
    dft_scale = float(1.0 / np.sqrt(S * cg))
    r1, r2 = S // GRID_W, GRID_W
    ewg, ewu, ewd = (w.astype(BF16) for w in (expert_w_gate, expert_w_up, expert_w_down))

    h = x.reshape(T, D)
    xs = None
    for i in range(depth):
        j = i // 2
        last = i == depth - 1
        if i % 2 == 0:
            w_ab = matmul_f32(fourier_w_in[j], chan, out_dtype=BF16)
            q = fourier_in(h, norm_mix[i], w_ab, batch=B, r1=r1, r2=r2)
            h = fft_stage2_out(q, fourier_w_out[j].astype(BF16), h, batch=B, r1=r1, r2=r2, scale=dft_scale)
            h = ffn(h, norm_ffn[i], ffn_w_gate[j].astype(BF16), ffn_w_up[j].astype(BF16),
                    ffn_w_down[j].astype(BF16))
            if last:
                h = final_norm(h, final_norm_g)
        else:
            kw = (nat_rpb.shape[3] + 1) // 2
            q, k, v = qkv_proj(h, norm_mix[i], nat_w_qkv[j].astype(BF16), kw=kw)
            att = attention(q, k, v, nat_rpb[j], batch=B)
            h = matmul_res(att, nat_w_o[j].astype(BF16), h)
            hn, idx, gates = router(h, norm_ffn[i], router_w[j], router_b[j])
            tm = _moe_block_rows(T)
            dest, block_expert, n_used, n_rows = _route(idx, n_experts, tm)
            xs = dispatch_rows(hn, dest, n_rows, s8=D // LANES, init=xs)
            ys = expert_ffn(xs, block_expert, n_used, ewg, ewu, ewd, layer=j, tm=tm)
            h = combine(h, ys, dest, gates, final_norm_g if last else None)
    return h.reshape(B, S, D)
```

```python
import functools

import numpy as np
import jax
import jax.numpy as jnp
from jax import lax
from jax.experimental import pallas as pl
from jax.experimental.pallas import tpu as pltpu

F32 = jnp.float32
BF16 = jnp.bfloat16
I32 = jnp.int32

GRID_W = 64
N_FOURIER_GROUPS = 4
TOP_K = 2
RMS_EPS = 1e-6
MASK_VALUE = -1e30

LANES = 128
SUBLANES = 8

V7X_VMEM_BYTES = 64 * 1024 * 1024
VMEM_LIMIT_BYTES = V7X_VMEM_BYTES - 8 * 1024 * 1024


def _tile(n, target):
    if n <= target:
        return n
    for t in range(target, 0, -1):
        if n % t == 0:
            return t
    return n


def _params(*sem):
    return pltpu.CompilerParams(dimension_semantics=sem, vmem_limit_bytes=VMEM_LIMIT_BYTES)


def _rms(x, g):
    ms = jnp.mean(x * x, axis=-1, keepdims=True)
    return x * lax.rsqrt(ms + RMS_EPS) * g


def _dot(a, b):
    return jnp.dot(a, b, preferred_element_type=F32)


def _attn_col_plan(kw):
    W = GRID_W
    n_cb = W // kw
    wk = min(W, 2 * kw)
    starts = np.clip(np.arange(n_cb) * kw - kw // 2, 0, W - wk)
    c = np.arange(W)
    col_start = np.clip(c - kw // 2, 0, W - kw)
    blk = c // kw
    assert (col_start >= starts[blk]).all() and (col_start + kw <= starts[blk] + wk).all()
    return n_cb, wk, [int(s) for s in starts]


def _qkv_kernel(x_ref, g_ref, w_ref, q_ref, k_ref, v_ref, *, col_starts, wk):
    tm, d = q_ref.shape
    y = _dot(_rms(x_ref[...], g_ref[...]).astype(BF16), w_ref[...])
    q_ref[...] = y[:, :d].astype(BF16)
    n_cb = len(col_starts)
    for r in range(tm // GRID_W):
        for j, cs in enumerate(col_starts):
            src = slice(r * GRID_W + cs, r * GRID_W + cs + wk)
            dst = slice((r * n_cb + j) * wk, (r * n_cb + j + 1) * wk)
            k_ref[dst, :] = y[src, d:2 * d].astype(BF16)
            v_ref[dst, :] = y[src, 2 * d:].astype(BF16)


def qkv_proj(h, g, w, *, kw, tm=512):
    M, K = h.shape
    D = w.shape[1] // 3
    n_cb, wk, col_starts = _attn_col_plan(kw)
    tm = _tile(M, tm)
    assert tm % GRID_W == 0
    tkv = tm // GRID_W * n_cb * wk
    kv_rows = M // GRID_W * n_cb * wk
    return pl.pallas_call(
        functools.partial(_qkv_kernel, col_starts=col_starts, wk=wk),
        grid=(M // tm,),
        in_specs=[
            pl.BlockSpec((tm, K), lambda i: (i, 0)),
            pl.BlockSpec((1, K), lambda i: (0, 0)),
            pl.BlockSpec((K, 3 * D), lambda i: (0, 0)),
        ],
        out_specs=[
            pl.BlockSpec((tm, D), lambda i: (i, 0)),
            pl.BlockSpec((tkv, D), lambda i: (i, 0)),
            pl.BlockSpec((tkv, D), lambda i: (i, 0)),
        ],
        out_shape=[
            jax.ShapeDtypeStruct((M, D), BF16),
            jax.ShapeDtypeStruct((kv_rows, D), BF16),
            jax.ShapeDtypeStruct((kv_rows, D), BF16),
        ],
        compiler_params=_params("parallel"),
        name="qkv_proj",
    )(h, g.reshape(1, K), w)


def _matmul_res_kernel(a_ref, w_ref, r_ref, o_ref):
    o_ref[...] = r_ref[...] + _dot(a_ref[...], w_ref[...])


def matmul_res(a, w, res, *, tm=1024, tn=1024):
    M, K = a.shape
    N = w.shape[1]
    tm, tn = _tile(M, tm), _tile(N, tn)
    return pl.pallas_call(
        _matmul_res_kernel,
        grid=(M // tm, N // tn),
        in_specs=[
            pl.BlockSpec((tm, K), lambda i, j: (i, 0)),
            pl.BlockSpec((K, tn), lambda i, j: (0, j)),
            pl.BlockSpec((tm, tn), lambda i, j: (i, j)),
        ],
        out_specs=pl.BlockSpec((tm, tn), lambda i, j: (i, j)),
        out_shape=jax.ShapeDtypeStruct((M, N), F32),
        compiler_params=_params("parallel", "parallel"),
        name="matmul_res",
    )(a, w, res)


def _matmul_f32_kernel(a_ref, b_ref, o_ref):
    o_ref[...] = jnp.dot(a_ref[...], b_ref[...], preferred_element_type=F32,
                         precision=lax.Precision.HIGHEST).astype(o_ref.dtype)


def matmul_f32(a, b, *, out_dtype, tm=512, tn=512):
    M, K = a.shape
    N = b.shape[1]
    tm, tn = _tile(M, tm), _tile(N, tn)
    return pl.pallas_call(
        _matmul_f32_kernel,
        grid=(M // tm, N // tn),
        in_specs=[
            pl.BlockSpec((tm, K), lambda i, j: (i, 0)),
            pl.BlockSpec((K, tn), lambda i, j: (0, j)),
        ],
        out_specs=pl.BlockSpec((tm, tn), lambda i, j: (i, j)),
        out_shape=jax.ShapeDtypeStruct((M, N), out_dtype),
        compiler_params=_params("parallel", "parallel"),
        name="matmul_f32",
    )(a, b)


def _dft_np(n):
    j = np.arange(n)
    ang = 2.0 * np.pi * ((j[:, None] * j[None, :]) % n) / n
    return np.cos(ang), np.sin(ang)


def _fourier_in_kernel(x_ref, g_ref, w_ref, ga_ref, gb_ref, twc_ref, tws_ref, o_ref):
    r1, n_c, dm = x_ref.shape
    n = r1 * n_c
    xn = _rms(x_ref[...].reshape(n, dm), g_ref[...]).astype(BF16)
    ab = _dot(xn, w_ref[...])
    d = ab.shape[1] // 2
    a, b = ab[:, :d].astype(BF16), ab[:, d:].astype(BF16)
    p = _dot(ga_ref[...], a) + _dot(gb_ref[...], b)
    pr, pi = p[:n], p[n:]
    tc, ts = twc_ref[...], tws_ref[...]
    q = jnp.concatenate([pr * tc + pi * ts, pi * tc - pr * ts], axis=1)
    o_ref[...] = q.reshape(r1, n_c, 2 * d)


def fourier_in(h, g, w_ab, *, batch, r1, r2, cols_per_step=SUBLANES):
    T, D = h.shape
    D2 = w_ab.shape[1]
    n_c = _tile(r2, cols_per_step)
    n = r1 * n_c
    c1, s1 = _dft_np(r1)
    eye = np.eye(n_c)
    ga = jnp.asarray(np.kron(np.concatenate([c1, -s1], axis=0), eye), BF16)
    gb = jnp.asarray(np.kron(np.concatenate([-s1, -c1], axis=0), eye), BF16)
    c = np.arange(r2).reshape(r2 // n_c, 1, n_c)
    kc = np.arange(r1).reshape(1, r1, 1)
    ang = (2.0 * np.pi * (c * kc) / (r1 * r2)).reshape(r2 // n_c, n, 1)
    twc, tws = jnp.asarray(np.cos(ang), F32), jnp.asarray(np.sin(ang), F32)
    out = pl.pallas_call(
        _fourier_in_kernel,
        grid=(batch, r2 // n_c),
        in_specs=[
            pl.BlockSpec((None, r1, n_c, D), lambda b, c: (b, 0, c, 0)),
            pl.BlockSpec((1, D), lambda b, c: (0, 0)),
            pl.BlockSpec((D, D2), lambda b, c: (0, 0)),
            pl.BlockSpec((2 * n, n), lambda b, c: (0, 0)),
            pl.BlockSpec((2 * n, n), lambda b, c: (0, 0)),
            pl.BlockSpec((None, n, 1), lambda b, c: (c, 0, 0)),
            pl.BlockSpec((None, n, 1), lambda b, c: (c, 0, 0)),
        ],
        out_specs=pl.BlockSpec((None, r1, n_c, D2), lambda b, c: (b, 0, c, 0)),
        out_shape=jax.ShapeDtypeStruct((batch, r1, r2, D2), F32),
        compiler_params=_params("parallel", "parallel"),
        name="fourier_in",
    )(h.reshape(batch, r1, r2, D), g.reshape(1, D), w_ab, ga, gb, twc, tws)
    return out.reshape(T, D2)


def _fft_stage2_kernel(c_ref, s_ref, q_ref, w_ref, r_ref, o_ref, *, scale):
    kb, r2, d2 = q_ref.shape
    d = d2 // 2
    fs = []
    for j in range(kb):
        qr, qi = q_ref[j, :, :d].astype(BF16), q_ref[j, :, d:].astype(BF16)
        f = _dot(c_ref[...], qr) + _dot(s_ref[...], qi)
        fs.append((f * scale).astype(BF16))
    g = _dot(jnp.concatenate(fs, axis=0), w_ref[...])
    for j in range(kb):
        o_ref[:, j, :] = r_ref[:, j, :] + g[j * r2:(j + 1) * r2, :]


def fft_stage2_out(q, w_out, res, *, batch, r1, r2, scale, kc_per_step=8):
    T, D2 = q.shape
    D = D2 // 2
    kb = _tile(r1, kc_per_step)
    c2, s2 = _dft_np(r2)
    q4 = q.reshape(batch, r1, r2, D2)
    res4 = res.reshape(batch, r2, r1, D)
    out = pl.pallas_call(
        functools.partial(_fft_stage2_kernel, scale=scale),
        grid=(batch, r1 // kb),
        in_specs=[
            pl.BlockSpec((r2, r2), lambda b, k: (0, 0)),
            pl.BlockSpec((r2, r2), lambda b, k: (0, 0)),
            pl.BlockSpec((None, kb, r2, D2), lambda b, k: (b, k, 0, 0)),
            pl.BlockSpec((D, D), lambda b, k: (0, 0)),
            pl.BlockSpec((None, r2, kb, D), lambda b, k: (b, 0, k, 0)),
        ],
        out_specs=pl.BlockSpec((None, r2, kb, D), lambda b, k: (b, 0, k, 0)),
        out_shape=jax.ShapeDtypeStruct((batch, r2, r1, D), F32),
        compiler_params=_params("parallel", "parallel"),
        name="fft_stage2_out",
    )(jnp.asarray(c2, BF16), jnp.asarray(s2, BF16), q4, w_out, res4)
    return out.reshape(T, D)


def _ffn_kernel(x_ref, g_ref, wg_ref, wu_ref, wd_ref, o_ref):
    x = x_ref[...]
    xn = _rms(x, g_ref[...]).astype(BF16)
    a = (jax.nn.silu(_dot(xn, wg_ref[...])) * _dot(xn, wu_ref[...])).astype(BF16)
    o_ref[...] = x + _dot(a, wd_ref[...])


def ffn(h, g, wg, wu, wd, *, tm=512):
    M, D = h.shape
    F = wg.shape[1]
    tm = _tile(M, tm)
    return pl.pallas_call(
        _ffn_kernel,
        grid=(M // tm,),
        in_specs=[
            pl.BlockSpec((tm, D), lambda i: (i, 0)),
            pl.BlockSpec((1, D), lambda i: (0, 0)),
            pl.BlockSpec((D, F), lambda i: (0, 0)),
            pl.BlockSpec((D, F), lambda i: (0, 0)),
            pl.BlockSpec((F, D), lambda i: (0, 0)),
        ],
        out_specs=pl.BlockSpec((tm, D), lambda i: (i, 0)),
        out_shape=jax.ShapeDtypeStruct((M, D), F32),
        compiler_params=_params("parallel"),
        name="ffn",
    )(h, g.reshape(1, D), wg, wu, wd)


def _attn_kernel(ks_ref, tid_ref, dr_ref, q_ref, k_ref, v_ref, tab_ref, o_ref, b_ref, *, n_heads, scale):
    NB, R, C, D = q_ref.shape
    KR, WK = k_ref.shape[1], k_ref.shape[3]
    hd = D // n_heads
    nq = R * C
    nk = KR * WK

    i, bidx = pl.program_id(1), pl.program_id(2)
    tid = tid_ref[i]
    changed = (i == 0) | (tid != tid_ref[jnp.maximum(i - 1, 0)])

    @pl.when((bidx == 0) & changed)
    def _():
        def per_head(h, carry):
            for rq in range(R):
                for rk in range(KR):
                    dr = dr_ref[(tid * R + rq) * KR + rk]
                    b_ref[h, rq * C:(rq + 1) * C, rk * WK:(rk + 1) * WK] = tab_ref[h, dr]
            return carry

        lax.fori_loop(0, n_heads, per_head, 0)

    lanes = min(D, max(hd, LANES))
    per_group = lanes // hd
    fold_scale = float(np.frexp(scale)[0]) == 0.5
    lane = lax.broadcasted_iota(I32, (1, lanes), 1)
    n_groups = D // lanes
    masks = [(lane >= hh * hd) & (lane < (hh + 1) * hd) for hh in range(per_group)]

    logits = []
    for bb in range(NB):
        for g in range(n_groups):
            sl = slice(g * lanes, (g + 1) * lanes)
            q = q_ref[bb, :, :, sl].reshape(nq, lanes)
            k = k_ref[bb, :, 0, :, sl].reshape(nk, lanes)
            if fold_scale:
                q = q * scale
            for hh in range(per_group):
                qh = jnp.where(masks[hh], q, jnp.zeros_like(q)) if per_group > 1 else q
                s = lax.dot_general(qh, k, (((1,), (1,)), ((), ())), preferred_element_type=F32)
                if not fold_scale:
                    s = s * scale
                logits.append(s + b_ref[g * per_group + hh])
    probs, inv_ls = [], []
    for s in logits:
        p = jnp.exp(s - jnp.max(s, axis=-1, keepdims=True))
        inv_ls.append(1.0 / jnp.sum(p, axis=-1, keepdims=True))
        probs.append(p.astype(BF16))
    for bb in range(NB):
        for g in range(n_groups):
            sl = slice(g * lanes, (g + 1) * lanes)
            v = v_ref[bb, :, 0, :, sl].reshape(nk, lanes)
            out = None
            for hh in range(per_group):
                h = (bb * n_groups + g) * per_group + hh
                o = _dot(probs[h], v) * inv_ls[h]
                out = o if out is None else jnp.where(masks[hh], o, out)
            o_ref[bb, :, :, sl] = out.reshape(R, C, lanes).astype(o_ref.dtype)


def _attn_plan(rows, kh, R):
    KR = min(rows, R + kh)
    r = np.arange(rows)
    row_start = np.clip(r - kh // 2, 0, rows - kh)
    nblk = rows // R
    ks = np.clip(np.arange(nblk) * R - kh // 2, 0, rows - KR)
    sigs, tid = [], []
    for i in range(nblk):
        rs = row_start[i * R:(i + 1) * R]
        assert (rs >= ks[i]).all() and (rs + kh <= ks[i] + KR).all()
        sig = (i * R - ks[i],) + tuple(rs - ks[i])
        if sig not in sigs:
            sigs.append(sig)
        tid.append(sigs.index(sig))
    return KR, ks.astype(np.int32), np.asarray(tid, np.int32), sigs


def _attn_bias_table(rpb, kh, kw, R, KR, sigs, wk, col_starts):
    H = rpb.shape[0]
    win_h_max = (rpb.shape[1] + 1) // 2
    W = GRID_W
    c = np.arange(W)
    col_start = np.clip(c - kw // 2, 0, W - kw)
    kc = np.arange(W)
    col_ok = (kc[None, :] >= col_start[:, None]) & (kc[None, :] < col_start[:, None] + kw)
    dc = kc[None, :] - c[:, None] + (kw - 1)
    onehot = (dc[None] == np.arange(2 * kw - 1)[:, None, None]) & col_ok[None]
    tab = jnp.einsum("hrd,dck->hrck", rpb.astype(F32), jnp.asarray(onehot, F32),
                     precision=lax.Precision.HIGHEST)
    tab = jnp.where(jnp.asarray(col_ok), tab, MASK_VALUE)
    n_dr = 2 * win_h_max - 1
    tab = jnp.concatenate([tab, jnp.full((H, 1, W, W), MASK_VALUE, F32)], axis=1)
    tab = jnp.stack([tab[:, :, j * kw:(j + 1) * kw, cs:cs + wk] for j, cs in enumerate(col_starts)], axis=0)
    dr_of =np.full((len(sigs), R, KR), n_dr, np.int32)
    for t, sig in enumerate(sigs):
        q0, rs = sig[0], sig[1:]
        for rq in range(R):
            for rk in range(KR):
                if rs[rq] <= rk < rs[rq] + kh:
                    dr_of[t, rq, rk] = rk - (q0 + rq) + (win_h_max - 1)
    assert dr_of.min() >= 0
    return tab, dr_of.reshape(-1)


def attention(q, k, v, rpb, *, batch, rows_per_step=8, batch_per_step=4):
    T, D = q.shape
    nb = _tile(batch, batch_per_step)
    S = T // batch
    W = GRID_W
    rows = S // W
    H = rpb.shape[0]
    kh = min((rpb.shape[1] + 1) // 2, rows)
    kw = (rpb.shape[2] + 1) // 2
    hd = D // H
    n_cb, wk, col_starts = _attn_col_plan(kw)
    R = _tile(rows, rows_per_step)
    KR, ks, tid, sigs = _attn_plan(rows, kh, R)
    tab, dr_of = _attn_bias_table(rpb, kh, kw, R, KR, sigs, wk, col_starts)
    q5 = q.reshape(batch, rows, n_cb, kw, D)
    k5 = k.reshape(batch, rows, n_cb, wk, D)
    v5 = v.reshape(batch, rows, n_cb, wk, D)

    kv_spec = pl.BlockSpec(
        (pl.Element(nb), pl.Element(KR), pl.Element(1), pl.Element(wk), pl.Element(D)),
        lambda j, i, b, ks_ref, tid_ref, dr_ref: (b * nb, ks_ref[i], j, 0, 0))
    q_spec = pl.BlockSpec((nb, R, None, kw, D), lambda j, i, b, ks_ref, tid_ref, dr_ref: (b, i, j, 0, 0))

    out = pl.pallas_call(
        functools.partial(_attn_kernel, n_heads=H, scale=hd ** -0.5),
        grid_spec=pltpu.PrefetchScalarGridSpec(
            num_scalar_prefetch=3,
            grid=(n_cb, rows // R, batch // nb),
            in_specs=[
                q_spec,
                kv_spec,
                kv_spec,
                pl.BlockSpec((None,) + tab.shape[1:], lambda j, i, b, ks_ref, tid_ref, dr_ref: (j, 0, 0, 0, 0)),
            ],
            out_specs=q_spec,
            scratch_shapes=[pltpu.VMEM((H, R * kw, KR * wk), F32)],
        ),
        out_shape=jax.ShapeDtypeStruct((batch, rows, n_cb, kw, D), BF16),
        compiler_params=_params("arbitrary", "arbitrary", "arbitrary"),
        name="attention",
    )(jnp.asarray(ks), jnp.asarray(tid), jnp.asarray(dr_of), q5, k5, v5, tab)
    return out.reshape(T, D)


def _store_token_tiles(dst_ref, x):
    n, d = x.shape
    s8 = d // LANES
    for s in range(s8):
        dst_ref[pl.ds(s, n, stride=s8), :] = x[:, s * LANES:(s + 1) * LANES]


def _load_token_tiles(src_ref, first, n, s8):
    return jnp.concatenate(
        [src_ref[pl.ds(first * s8 + s, n, stride=s8), :] for s in range(s8)], axis=1)


def _token_copy(src_ref, dst_ref, sem, src_tok, dst_tok, s8):
    return pltpu.make_async_copy(
        src_ref.at[pl.ds(pl.multiple_of(src_tok * s8, s8), s8), :],
        dst_ref.at[pl.ds(pl.multiple_of(dst_tok * s8, s8), s8), :], sem)


def _check_token_tiles(d):
    assert d % (SUBLANES * LANES) == 0, "token-tile layout needs whole (8, 128) tiles per token"
    return d // LANES


def _router_kernel(x_ref, g_ref, wr_ref, br_ref, hn_ref, idx_ref, gate_ref):
    hn = _rms(x_ref[...], g_ref[...])
    _store_token_tiles(hn_ref, hn)
    w = wr_ref[...]
    hn_hi, w_hi = hn.astype(BF16), w.astype(BF16)
    hn_lo = (hn - hn_hi.astype(F32)).astype(BF16)
    w_lo = (w - w_hi.astype(F32)).astype(BF16)
    logits = _dot(jnp.concatenate([hn_hi, hn_lo, hn_hi], axis=1),
                  jnp.concatenate([w_hi, w_hi, w_lo], axis=0)) + br_ref[...]
    n_e = logits.shape[1]
    lane = lax.broadcasted_iota(I32, logits.shape, 1)
    v1 = jnp.max(logits, axis=-1, keepdims=True)
    i1 = jnp.min(jnp.where(logits == v1, lane, n_e), axis=-1, keepdims=True)
    rest = jnp.where(lane == i1, -jnp.inf, logits)
    v2 = jnp.max(rest, axis=-1, keepdims=True)
    i2 = jnp.min(jnp.where(rest == v2, lane, n_e), axis=-1, keepdims=True)
    e2 = jnp.exp(v2 - v1)
    first = lax.broadcasted_iota(I32, idx_ref.shape, 1) == 0
    idx_ref[...] = jnp.where(first, i1, i2)
    gate_ref[...] = jnp.where(first, 1.0, e2) / (1.0 + e2)


def router(h, g, wr, br, *, tm=1024):
    M, D = h.shape
    E = wr.shape[1]
    s8 = _check_token_tiles(D)
    tm = _tile(M, tm)
    return pl.pallas_call(
        _router_kernel,
        grid=(M // tm,),
        in_specs=[
            pl.BlockSpec((tm, D), lambda i: (i, 0)),
            pl.BlockSpec((1, D), lambda i: (0, 0)),
            pl.BlockSpec((D, E), lambda i: (0, 0)),
            pl.BlockSpec((1, E), lambda i: (0, 0)),
        ],
        out_specs=[
            pl.BlockSpec((tm * s8, LANES), lambda i: (i, 0)),
            pl.BlockSpec((tm, TOP_K), lambda i: (i, 0)),
            pl.BlockSpec((tm, TOP_K), lambda i: (i, 0)),
        ],
        out_shape=[
            jax.ShapeDtypeStruct((M * s8, LANES), F32),
            jax.ShapeDtypeStruct((M, TOP_K), I32),
            jax.ShapeDtypeStruct((M, TOP_K), F32),
        ],
        compiler_params=_params("parallel"),
        name="router",
    )(h, g.reshape(1, D), wr, br.reshape(1, E))


def _dispatch_kernel(dest_ref, hn_ref, xs_in_hbm, xs_hbm, sem, *, s8):
    del xs_in_hbm
    tq = hn_ref.shape[0] // s8

    def issue(t, carry):
        for k in range(TOP_K):
            _token_copy(hn_ref, xs_hbm, sem, t, dest_ref[TOP_K * t + k], s8).start()
        return carry

    lax.fori_loop(0, tq, issue, 0)
    for _ in range(TOP_K):
        pltpu.make_async_copy(hn_ref, xs_hbm.at[pl.ds(0, tq * s8), :], sem).wait()


def dispatch_rows(hn, dest, n_rows, *, s8, init=None, tq=512):
    if init is None:
        init = jnp.zeros((n_rows * s8, LANES), hn.dtype)
    T = hn.shape[0] // s8
    tq = _tile(T, tq)
    return pl.pallas_call(
        functools.partial(_dispatch_kernel, s8=s8),
        grid=(T // tq,),
        in_specs=[
            pl.BlockSpec((TOP_K * tq,), lambda i: (i,), memory_space=pltpu.SMEM),
            pl.BlockSpec((tq * s8, LANES), lambda i: (i, 0)),
            pl.BlockSpec(memory_space=pl.ANY),
        ],
        out_specs=pl.BlockSpec(memory_space=pl.ANY),
        out_shape=jax.ShapeDtypeStruct((n_rows * s8, LANES), hn.dtype),
        scratch_shapes=[pltpu.SemaphoreType.DMA],
        input_output_aliases={2: 0},
        compiler_params=_params("arbitrary"),
        name="dispatch_rows",
    )(dest, hn, init)


def _expert_kernel(be_ref, nu_ref, x_ref, wg_ref, wu_ref, wd_ref, o_ref):
    used = pl.program_id(0) < nu_ref[0]
    d = wg_ref.shape[0]
    s8 = d // LANES
    tm = x_ref.shape[0] // s8

    @pl.when(used)
    def _():
        x = _load_token_tiles(x_ref, 0, tm, s8).astype(BF16)
        a = (jax.nn.silu(_dot(x, wg_ref[...])) * _dot(x, wu_ref[...])).astype(BF16)
        _store_token_tiles(o_ref, _dot(a, wd_ref[...]))

    @pl.when(jnp.logical_not(used))
    def _():
        o_ref[...] = jnp.zeros_like(o_ref)


def expert_ffn(xs, block_expert, n_used, wg, wu, wd, *, layer, tm):
    D, F = wg.shape[2], wg.shape[3]
    s8 = D // LANES
    R = xs.shape[0] // s8
    return pl.pallas_call(
        _expert_kernel,
        grid_spec=pltpu.PrefetchScalarGridSpec(
            num_scalar_prefetch=2,
            grid=(R // tm,),
            in_specs=[
                pl.BlockSpec((tm * s8, LANES), lambda i, be, nu: (i, 0)),
                pl.BlockSpec((None, None, D, F), lambda i, be, nu: (layer, be[i], 0, 0)),
                pl.BlockSpec((None, None, D, F), lambda i, be, nu: (layer, be[i], 0, 0)),
                pl.BlockSpec((None, None, F, D), lambda i, be, nu: (layer, be[i], 0, 0)),
            ],
            out_specs=pl.BlockSpec((tm * s8, LANES), lambda i, be, nu: (i, 0)),
        ),
        out_shape=jax.ShapeDtypeStruct((R * s8, LANES), F32),
        compiler_params=_params("parallel"),
        name="expert_ffn",
    )(block_expert, n_used, xs, wg, wu, wd)


def _combine_kernel(dest_ref, dest_next_ref, ys_hbm, h_ref, gate_ref, g_ref, o_ref, buf_ref, sem, *, final_norm):
    tq, d = h_ref.shape
    s8 = d // LANES
    i = pl.program_id(0)
    cur = i % 2

    def gather(d_ref, slot):
        def issue(t, carry):
            for k in range(TOP_K):
                _token_copy(ys_hbm, buf_ref.at[slot], sem.at[slot], d_ref[TOP_K * t + k], k * tq + t, s8).start()
            return carry

        lax.fori_loop(0, tq, issue, 0)

    @pl.when(i == 0)
    def _():
        gather(dest_ref, 0)

    @pl.when(i + 1 < pl.num_programs(0))
    def _():
        gather(dest_next_ref, 1 - cur)

    pltpu.make_async_copy(ys_hbm.at[pl.ds(0, TOP_K * tq * s8), :], buf_ref.at[cur], sem.at[cur]).wait()

    def rows(k):
        return jnp.concatenate(
            [buf_ref[cur, pl.ds(k * tq * s8 + s, tq, stride=s8), :] for s in range(s8)], axis=1)

    gate = gate_ref[...]
    ff = rows(0) * gate[:, 0:1]
    for k in range(1, TOP_K):
        ff = ff + rows(k) * gate[:, k:k + 1]
    out = h_ref[...] + ff
    if final_norm:
        out = _rms(out, g_ref[...])
    o_ref[...] = out


def combine(h, ys, dest, gates, final_gain, *, tq=512):
    T, D = h.shape
    s8 = D // LANES
    tq = _tile(T, tq)
    final_norm = final_gain is not None
    gain = final_gain if final_norm else jnp.ones((D,), F32)
    n_tiles = T // tq
    return pl.pallas_call(
        functools.partial(_combine_kernel, final_norm=final_norm),
        grid=(n_tiles,),
        in_specs=[
            pl.BlockSpec((TOP_K * tq,), lambda i: (i,), memory_space=pltpu.SMEM),
            pl.BlockSpec((TOP_K * tq,), lambda i: (jnp.minimum(i + 1, n_tiles - 1),), memory_space=pltpu.SMEM),
            pl.BlockSpec(memory_space=pl.ANY),
            pl.BlockSpec((tq, D), lambda i: (i, 0)),
            pl.BlockSpec((tq, TOP_K), lambda i: (i, 0)),
            pl.BlockSpec((1, D), lambda i: (0, 0)),
        ],
        out_specs=pl.BlockSpec((tq, D), lambda i: (i, 0)),
        out_shape=jax.ShapeDtypeStruct((T, D), F32),
        scratch_shapes=[pltpu.VMEM((2, TOP_K * tq * s8, LANES), F32), pltpu.SemaphoreType.DMA((2,))],
        compiler_params=_params("arbitrary"),
        name="combine",
    )(dest, dest, ys, h, gates, gain.reshape(1, D))


def _final_norm_kernel(x_ref, g_ref, o_ref):
    o_ref[...] = _rms(x_ref[...], g_ref[...])


def final_norm(h, g, *, tm=1024):
    M, D = h.shape
    tm = _tile(M, tm)
    return pl.pallas_call(
        _final_norm_kernel,
        grid=(M // tm,),
        in_specs=[pl.BlockSpec((tm, D), lambda i: (i, 0)), pl.BlockSpec((1, D), lambda i: (0, 0))],
        out_specs=pl.BlockSpec((tm, D), lambda i: (i, 0)),
        out_shape=jax.ShapeDtypeStruct((M, D), F32),
        compiler_params=_params("parallel"),
        name="final_norm",
    )(h, g.reshape(1, D))


def _dft_cos_sin(n):
    j = lax.broadcasted_iota(I32, (n, n), 0)
    k = lax.broadcasted_iota(I32, (n, n), 1)
    ang = ((j * k) % n).astype(F32) * (2.0 * np.pi / n)
    return jnp.cos(ang), jnp.sin(ang)


def _route(idx, n_experts, tm):
    T = idx.shape[0]
    TK = T * TOP_K
    flat_e = idx.reshape(TK)
    onehot = (flat_e[:, None] == jnp.arange(n_experts, dtype=I32)[None, :]).astype(I32)
    csum = jnp.cumsum(onehot, axis=0)
    rank = jnp.sum(csum * onehot, axis=1) - 1
    counts = csum[-1]
    padded = (counts + tm - 1) // tm * tm
    pend = jnp.cumsum(padded)
    pstart = pend - padded
    dest = (pstart[flat_e] + rank).astype(I32)
    n_blocks = -(-(TK + n_experts * (tm - 1)) // tm)
    block_expert = jnp.minimum(
        jnp.searchsorted(pend, jnp.arange(n_blocks, dtype=I32) * tm, side="right"), n_experts - 1).astype(I32)
    n_used = (pend[-1:] // tm).astype(I32)
    return dest, block_expert, n_used, n_blocks * tm


def _moe_block_rows(T):
    return _tile(T * TOP_K, 256)


def kernel(x, norm_mix, norm_ffn, fourier_w_in, fourier_w_out, nat_w_qkv, nat_rpb, nat_w_o,
           ffn_w_gate, ffn_w_up, ffn_w_down, router_w, router_b,
           expert_w_gate, expert_w_up, expert_w_down, final_norm_g):
    B, S, D = x.shape
    T = B * S
    depth = norm_mix.shape[0]
    n_experts = router_w.shape[2]
    cg = D // N_FOURIER_GROUPS

    cc, sc = _dft_cos_sin(cg)
    eye = jnp.eye(N_FOURIER_GROUPS, dtype=F32)
    chan = jnp.concatenate([jnp.kron(eye, cc), jnp.kron(eye, sc)], axis=1)
    dft_scale = float(1.0 / np.sqrt(S * cg))
    r1, r2 = S // GRID_W, GRID_W
    ewg, ewu, ewd = (w.astype(BF16) for w in (expert_w_gate, expert_w_up, expert_w_down))

    h = x.reshape(T, D)
    xs = None
    for i in range(depth):
        j = i // 2
        last = i == depth - 1
        if i % 2 == 0:
            w_ab = matmul_f32(fourier_w_in[j], chan, out_dtype=BF16)
            q = fourier_in(h, norm_mix[i], w_ab, batch=B, r1=r1, r2=r2)
            h = fft_stage2_out(q, fourier_w_out[j].astype(BF16), h, batch=B, r1=r1, r2=r2, scale=dft_scale)
            h = ffn(h, norm_ffn[i], ffn_w_gate[j].astype(BF16), ffn_w_up[j].astype(BF16),
                    ffn_w_down[j].astype(BF16))
            if last:
                h = final_norm(h, final_norm_g)
        else:
            kw = (nat_rpb.shape[3] + 1) // 2
            q, k, v = qkv_proj(h, norm_mix[i], nat_w_qkv[j].astype(BF16), kw=kw)
            att = attention(q, k, v, nat_rpb[j], batch=B)
            h = matmul_res(att, nat_w_o[j].astype(BF16), h)
            hn, idx, gates = router(h, norm_ffn[i], router_w[j], router_b[j])
            tm = _moe_block_rows(T)
            dest, block_expert, n_used, n_rows = _route(idx, n_experts, tm)
            xs = dispatch_rows(hn, dest, n_rows, s8=D // LANES, init=xs)
            ys = expert_ffn(xs, block_expert, n_used, ewg, ewu, ewd, layer=j, tm=tm)
            h = combine(h, ys, dest, gates, final_norm_g if last else None)
    return h.reshape(B, S, D)
```

```python
import functools

import numpy as np
import jax
import jax.numpy as jnp
from jax import lax
from jax.experimental import pallas as pl
from jax.experimental.pallas import tpu as pltpu

F32 = jnp.float32
BF16 = jnp.bfloat16
I32 = jnp.int32

GRID_W = 64
N_FOURIER_GROUPS = 4
TOP_K = 2
RMS_EPS = 1e-6
MASK_VALUE = -1e30

LANES = 128
SUBLANES = 8

V7X_VMEM_BYTES = 64 * 1024 * 1024
VMEM_LIMIT_BYTES = V7X_VMEM_BYTES - 8 * 1024 * 1024


def _tile(n, target):
    if n <= target:
        return n
    for t in range(target, 0, -1):
        if n % t == 0:
            return t
    return n


def _params(*sem):
    return pltpu.CompilerParams(dimension_semantics=sem, vmem_limit_bytes=VMEM_LIMIT_BYTES)


def _rms(x, g):
    ms = jnp.mean(x * x, axis=-1, keepdims=True)
    return x * lax.rsqrt(ms + RMS_EPS) * g


def _dot(a, b):
    return jnp.dot(a, b, preferred_element_type=F32)


def _attn_col_plan(kw):
    W = GRID_W
    n_cb = W // kw
    wk = min(W, 2 * kw)
    starts = np.clip(np.arange(n_cb) * kw - kw // 2, 0, W - wk)
    c = np.arange(W)
    col_start = np.clip(c - kw // 2, 0, W - kw)
    blk = c // kw
    assert (col_start >= starts[blk]).all() and (col_start + kw <= starts[blk] + wk).all()
    return n_cb, wk, [int(s) for s in starts]


def _qkv_kernel(x_ref, g_ref, w_ref, q_ref, k_ref, v_ref, *, col_starts, wk):
    tm, d = q_ref.shape
    y = _dot(_rms(x_ref[...], g_ref[...]).astype(BF16), w_ref[...])
    q_ref[...] = y[:, :d].astype(BF16)
    n_cb = len(col_starts)
    for r in range(tm // GRID_W):
        for j, cs in enumerate(col_starts):
            src = slice(r * GRID_W + cs, r * GRID_W + cs + wk)
            dst = slice((r * n_cb + j) * wk, (r * n_cb + j + 1) * wk)
            k_ref[dst, :] = y[src, d:2 * d].astype(BF16)
            v_ref[dst, :] = y[src, 2 * d:].astype(BF16)


def qkv_proj(h, g, w, *, kw, tm=512):
    M, K = h.shape
    D = w.shape[1] // 3
    n_cb, wk, col_starts = _attn_col_plan(kw)
    tm = _tile(M, tm)
    assert tm % GRID_W == 0
    tkv = tm // GRID_W * n_cb * wk
    kv_rows = M // GRID_W * n_cb * wk
    return pl.pallas_call(
        functools.partial(_qkv_kernel, col_starts=col_starts, wk=wk),
        grid=(M // tm,),
        in_specs=[
            pl.BlockSpec((tm, K), lambda i: (i, 0)),
            pl.BlockSpec((1, K), lambda i: (0, 0)),
            pl.BlockSpec((K, 3 * D), lambda i: (0, 0)),
        ],
        out_specs=[
            pl.BlockSpec((tm, D), lambda i: (i, 0)),
            pl.BlockSpec((tkv, D), lambda i: (i, 0)),
            pl.BlockSpec((tkv, D), lambda i: (i, 0)),
        ],
        out_shape=[
            jax.ShapeDtypeStruct((M, D), BF16),
            jax.ShapeDtypeStruct((kv_rows, D), BF16),
            jax.ShapeDtypeStruct((kv_rows, D), BF16),
        ],
        compiler_params=_params("parallel"),
        name="qkv_proj",
    )(h, g.reshape(1, K), w)


def _matmul_res_kernel(a_ref, w_ref, r_ref, o_ref):
    o_ref[...] = r_ref[...] + _dot(a_ref[...], w_ref[...])


def matmul_res(a, w, res, *, tm=1024, tn=1024):
    M, K = a.shape
    N = w.shape[1]
    tm, tn = _tile(M, tm), _tile(N, tn)
    return pl.pallas_call(
        _matmul_res_kernel,
        grid=(M // tm, N // tn),
        in_specs=[
            pl.BlockSpec((tm, K), lambda i, j: (i, 0)),
            pl.BlockSpec((K, tn), lambda i, j: (0, j)),
            pl.BlockSpec((tm, tn), lambda i, j: (i, j)),
        ],
        out_specs=pl.BlockSpec((tm, tn), lambda i, j: (i, j)),
        out_shape=jax.ShapeDtypeStruct((M, N), F32),
        compiler_params=_params("parallel", "parallel"),
        name="matmul_res",
    )(a, w, res)


def _split_bf16(x):
    hi = x.astype(BF16)
    return hi, (x - hi.astype(F32)).astype(BF16)


def _matmul_f32_kernel(a_ref, b_ref, o_ref):
    a_hi, a_lo = _split_bf16(a_ref[...])
    b_hi, b_lo = _split_bf16(b_ref[...])
    o_ref[...] = _dot(jnp.concatenate([a_hi, a_lo, a_hi], axis=1),
                      jnp.concatenate([b_hi, b_hi, b_lo], axis=0)).astype(o_ref.dtype)


def matmul_f32(a, b, *, out_dtype, tm=512, tn=512):
    M, K = a.shape
    N = b.shape[1]
    tm, tn = _tile(M, tm), _tile(N, tn)
    return pl.pallas_call(
        _matmul_f32_kernel,
        grid=(M // tm, N // tn),
        in_specs=[
            pl.BlockSpec((tm, K), lambda i, j: (i, 0)),
            pl.BlockSpec((K, tn), lambda i, j: (0, j)),
        ],
        out_specs=pl.BlockSpec((tm, tn), lambda i, j: (i, j)),
        out_shape=jax.ShapeDtypeStruct((M, N), out_dtype),
        compiler_params=_params("parallel", "parallel"),
        name="matmul_f32",
    )(a, b)


def _dft_np(n):
    j = np.arange(n)
    ang = 2.0 * np.pi * ((j[:, None] * j[None, :]) % n) / n
    return np.cos(ang), np.sin(ang)


def _fourier_in_kernel(x_ref, g_ref, w_ref, ga_ref, gb_ref, twc_ref, tws_ref, o_ref):
    r1, n_c, dm = x_ref.shape
    n = r1 * n_c
    xn = _rms(x_ref[...].reshape(n, dm), g_ref[...]).astype(BF16)
    ab = _dot(xn, w_ref[...])
    d = ab.shape[1] // 2
    a, b = ab[:, :d].astype(BF16), ab[:, d:].astype(BF16)
    p = _dot(ga_ref[...], a) + _dot(gb_ref[...], b)
    pr, pi = p[:n], p[n:]
    tc, ts = twc_ref[...], tws_ref[...]
    q = jnp.concatenate([pr * tc + pi * ts, pi * tc - pr * ts], axis=1)
    o_ref[...] = q.reshape(r1, n_c, 2 * d)


def fourier_in(h, g, w_ab, *, batch, r1, r2, cols_per_step=SUBLANES):
    T, D = h.shape
    D2 = w_ab.shape[1]
    n_c = _tile(r2, cols_per_step)
    n = r1 * n_c
    c1, s1 = _dft_np(r1)
    eye = np.eye(n_c)
    ga = jnp.asarray(np.kron(np.concatenate([c1, -s1], axis=0), eye), BF16)
    gb = jnp.asarray(np.kron(np.concatenate([-s1, -c1], axis=0), eye), BF16)
    c = np.arange(r2).reshape(r2 // n_c, 1, n_c)
    kc = np.arange(r1).reshape(1, r1, 1)
    ang = (2.0 * np.pi * (c * kc) / (r1 * r2)).reshape(r2 // n_c, n, 1)
    twc, tws = jnp.asarray(np.cos(ang), F32), jnp.asarray(np.sin(ang), F32)
    out = pl.pallas_call(
        _fourier_in_kernel,
        grid=(batch, r2 // n_c),
        in_specs=[
            pl.BlockSpec((None, r1, n_c, D), lambda b, c: (b, 0, c, 0)),
            pl.BlockSpec((1, D), lambda b, c: (0, 0)),
            pl.BlockSpec((D, D2), lambda b, c: (0, 0)),
            pl.BlockSpec((2 * n, n), lambda b, c: (0, 0)),
            pl.BlockSpec((2 * n, n), lambda b, c: (0, 0)),
            pl.BlockSpec((None, n, 1), lambda b, c: (c, 0, 0)),
            pl.BlockSpec((None, n, 1), lambda b, c: (c, 0, 0)),
        ],
        out_specs=pl.BlockSpec((None, r1, n_c, D2), lambda b, c: (b, 0, c, 0)),
        out_shape=jax.ShapeDtypeStruct((batch, r1, r2, D2), F32),
        compiler_params=_params("parallel", "parallel"),
        name="fourier_in",
    )(h.reshape(batch, r1, r2, D), g.reshape(1, D), w_ab, ga, gb, twc, tws)
    return out.reshape(T, D2)


def _fft_stage2_kernel(c_ref, s_ref, q_ref, w_ref, r_ref, o_ref, *, scale):
    kb, r2, d2 = q_ref.shape
    d = d2 // 2
    fs = []
    for j in range(kb):
        qr, qi = q_ref[j, :, :d].astype(BF16), q_ref[j, :, d:].astype(BF16)
        f = _dot(c_ref[...], qr) + _dot(s_ref[...], qi)
        fs.append((f * scale).astype(BF16))
    g = _dot(jnp.concatenate(fs, axis=0), w_ref[...])
    for j in range(kb):
        o_ref[:, j, :] = r_ref[:, j, :] + g[j * r2:(j + 1) * r2, :]


def fft_stage2_out(q, w_out, res, *, batch, r1, r2, scale, kc_per_step=8):
    T, D2 = q.shape
    D = D2 // 2
    kb = _tile(r1, kc_per_step)
    c2, s2 = _dft_np(r2)
    q4 = q.reshape(batch, r1, r2, D2)
    res4 = res.reshape(batch, r2, r1, D)
    out = pl.pallas_call(
        functools.partial(_fft_stage2_kernel, scale=scale),
        grid=(batch, r1 // kb),
        in_specs=[
            pl.BlockSpec((r2, r2), lambda b, k: (0, 0)),
            pl.BlockSpec((r2, r2), lambda b, k: (0, 0)),
            pl.BlockSpec((None, kb, r2, D2), lambda b, k: (b, k, 0, 0)),
            pl.BlockSpec((D, D), lambda b, k: (0, 0)),
            pl.BlockSpec((None, r2, kb, D), lambda b, k: (b, 0, k, 0)),
        ],
        out_specs=pl.BlockSpec((None, r2, kb, D), lambda b, k: (b, 0, k, 0)),
        out_shape=jax.ShapeDtypeStruct((batch, r2, r1, D), F32),
        compiler_params=_params("parallel", "parallel"),
        name="fft_stage2_out",
    )(jnp.asarray(c2, BF16), jnp.asarray(s2, BF16), q4, w_out, res4)
    return out.reshape(T, D)


def _ffn_kernel(x_ref, g_ref, wg_ref, wu_ref, wd_ref, o_ref):
    x = x_ref[...]
    xn = _rms(x, g_ref[...]).astype(BF16)
    a = (jax.nn.silu(_dot(xn, wg_ref[...])) * _dot(xn, wu_ref[...])).astype(BF16)
    o_ref[...] = x + _dot(a, wd_ref[...])


def ffn(h, g, wg, wu, wd, *, tm=512):
    M, D = h.shape
    F = wg.shape[1]
    tm = _tile(M, tm)
    return pl.pallas_call(
        _ffn_kernel,
        grid=(M // tm,),
        in_specs=[
            pl.BlockSpec((tm, D), lambda i: (i, 0)),
            pl.BlockSpec((1, D), lambda i: (0, 0)),
            pl.BlockSpec((D, F), lambda i: (0, 0)),
            pl.BlockSpec((D, F), lambda i: (0, 0)),
            pl.BlockSpec((F, D), lambda i: (0, 0)),
        ],
        out_specs=pl.BlockSpec((tm, D), lambda i: (i, 0)),
        out_shape=jax.ShapeDtypeStruct((M, D), F32),
        compiler_params=_params("parallel"),
        name="ffn",
    )(h, g.reshape(1, D), wg, wu, wd)


def _attn_kernel(ks_ref, tid_ref, dr_ref, q_ref, k_ref, v_ref, tab_ref, o_ref, b_ref, *, n_heads, scale):
    NB, R, C, D = q_ref.shape
    KR, WK = k_ref.shape[1], k_ref.shape[3]
    hd = D // n_heads
    nq = R * C
    nk = KR * WK

    i, bidx = pl.program_id(1), pl.program_id(2)
    tid = tid_ref[i]
    changed = (i == 0) | (tid != tid_ref[jnp.maximum(i - 1, 0)])

    @pl.when((bidx == 0) & changed)
    def _():
        def per_head(h, carry):
            for rq in range(R):
                for rk in range(KR):
                    dr = dr_ref[(tid * R + rq) * KR + rk]
                    b_ref[h, rq * C:(rq + 1) * C, rk * WK:(rk + 1) * WK] = tab_ref[h, dr]
            return carry

        lax.fori_loop(0, n_heads, per_head, 0)

    lanes = min(D, max(hd, LANES))
    per_group = lanes // hd
    fold_scale = float(np.frexp(scale)[0]) == 0.5
    lane = lax.broadcasted_iota(I32, (1, lanes), 1)
    n_groups = D // lanes
    masks = [(lane >= hh * hd) & (lane < (hh + 1) * hd) for hh in range(per_group)]

    logits = []
    for bb in range(NB):
        for g in range(n_groups):
            sl = slice(g * lanes, (g + 1) * lanes)
            q = q_ref[bb, :, :, sl].reshape(nq, lanes)
            k = k_ref[bb, :, 0, :, sl].reshape(nk, lanes)
            if fold_scale:
                q = q * scale
            for hh in range(per_group):
                qh = jnp.where(masks[hh], q, jnp.zeros_like(q)) if per_group > 1 else q
                s = lax.dot_general(qh, k, (((1,), (1,)), ((), ())), preferred_element_type=F32)
                if not fold_scale:
                    s = s * scale
                logits.append(s + b_ref[g * per_group + hh])
    probs, inv_ls = [], []
    for s in logits:
        p = jnp.exp(s - jnp.max(s, axis=-1, keepdims=True))
        inv_ls.append(1.0 / jnp.sum(p, axis=-1, keepdims=True))
        probs.append(p.astype(BF16))
    for bb in range(NB):
        for g in range(n_groups):
            sl = slice(g * lanes, (g + 1) * lanes)
            v = v_ref[bb, :, 0, :, sl].reshape(nk, lanes)
            out = None
            for hh in range(per_group):
                h = (bb * n_groups + g) * per_group + hh
                o = _dot(probs[h], v) * inv_ls[h]
                out = o if out is None else jnp.where(masks[hh], o, out)
            o_ref[bb, :, :, sl] = out.reshape(R, C, lanes).astype(o_ref.dtype)


def _attn_plan(rows, kh, R):
    KR = min(rows, R + kh)
    r = np.arange(rows)
    row_start = np.clip(r - kh // 2, 0, rows - kh)
    nblk = rows // R
    ks = np.clip(np.arange(nblk) * R - kh // 2, 0, rows - KR)
    sigs, tid = [], []
    for i in range(nblk):
        rs = row_start[i * R:(i + 1) * R]
        assert (rs >= ks[i]).all() and (rs + kh <= ks[i] + KR).all()
        sig = (i * R - ks[i],) + tuple(rs - ks[i])
        if sig not in sigs:
            sigs.append(sig)
        tid.append(sigs.index(sig))
    return KR, ks.astype(np.int32), np.asarray(tid, np.int32), sigs


def _attn_bias_table(rpb, kh, kw, R, KR, sigs, wk, col_starts):
    H = rpb.shape[0]
    win_h_max = (rpb.shape[1] + 1) // 2
    W = GRID_W
    c = np.arange(W)
    col_start = np.clip(c - kw // 2, 0, W - kw)
    kc = np.arange(W)
    col_ok = (kc[None, :] >= col_start[:, None]) & (kc[None, :] < col_start[:, None] + kw)
    dc = kc[None, :] - c[:, None] + (kw - 1)
    onehot = (dc[None] == np.arange(2 * kw - 1)[:, None, None]) & col_ok[None]
    tab = jnp.einsum("hrd,dck->hrck", rpb.astype(F32), jnp.asarray(onehot, F32),
                     precision=lax.Precision.HIGHEST)
    tab = jnp.where(jnp.asarray(col_ok), tab, MASK_VALUE)
    n_dr = 2 * win_h_max - 1
    tab = jnp.concatenate([tab, jnp.full((H, 1, W, W), MASK_VALUE, F32)], axis=1)
    tab = jnp.stack([tab[:, :, j * kw:(j + 1) * kw, cs:cs + wk] for j, cs in enumerate(col_starts)], axis=0)
    dr_of =np.full((len(sigs), R, KR), n_dr, np.int32)
    for t, sig in enumerate(sigs):
        q0, rs = sig[0], sig[1:]
        for rq in range(R):
            for rk in range(KR):
                if rs[rq] <= rk < rs[rq] + kh:
                    dr_of[t, rq, rk] = rk - (q0 + rq) + (win_h_max - 1)
    assert dr_of.min() >= 0
    return tab, dr_of.reshape(-1)


def attention(q, k, v, rpb, *, batch, rows_per_step=8, batch_per_step=4):
    T, D = q.shape
    nb = _tile(batch, batch_per_step)
    S = T // batch
    W = GRID_W
    rows = S // W
    H = rpb.shape[0]
    kh = min((rpb.shape[1] + 1) // 2, rows)
    kw = (rpb.shape[2] + 1) // 2
    hd = D // H
    n_cb, wk, col_starts = _attn_col_plan(kw)
    R = _tile(rows, rows_per_step)
    KR, ks, tid, sigs = _attn_plan(rows, kh, R)
    tab, dr_of = _attn_bias_table(rpb, kh, kw, R, KR, sigs, wk, col_starts)
    q5 = q.reshape(batch, rows, n_cb, kw, D)
    k5 = k.reshape(batch, rows, n_cb, wk, D)
    v5 = v.reshape(batch, rows, n_cb, wk, D)

    kv_spec = pl.BlockSpec(
        (pl.Element(nb), pl.Element(KR), pl.Element(1), pl.Element(wk), pl.Element(D)),
        lambda j, i, b, ks_ref, tid_ref, dr_ref: (b * nb, ks_ref[i], j, 0, 0))
    q_spec = pl.BlockSpec((nb, R, None, kw, D), lambda j, i, b, ks_ref, tid_ref, dr_ref: (b, i, j, 0, 0))

    out = pl.pallas_call(
        functools.partial(_attn_kernel, n_heads=H, scale=hd ** -0.5),
        grid_spec=pltpu.PrefetchScalarGridSpec(
            num_scalar_prefetch=3,
            grid=(n_cb, rows // R, batch // nb),
            in_specs=[
                q_spec,
                kv_spec,
                kv_spec,
                pl.BlockSpec((None,) + tab.shape[1:], lambda j, i, b, ks_ref, tid_ref, dr_ref: (j, 0, 0, 0, 0)),
            ],
            out_specs=q_spec,
            scratch_shapes=[pltpu.VMEM((H, R * kw, KR * wk), F32)],
        ),
        out_shape=jax.ShapeDtypeStruct((batch, rows, n_cb, kw, D), BF16),
        compiler_params=_params("arbitrary", "arbitrary", "arbitrary"),
        name="attention",
    )(jnp.asarray(ks), jnp.asarray(tid), jnp.asarray(dr_of), q5, k5, v5, tab)
    return out.reshape(T, D)


def _store_token_tiles(dst_ref, x):
    n, d = x.shape
    s8 = d // LANES
    for s in range(s8):
        dst_ref[pl.ds(s, n, stride=s8), :] = x[:, s * LANES:(s + 1) * LANES]


def _load_token_tiles(src_ref, first, n, s8):
    return jnp.concatenate(
        [src_ref[pl.ds(first * s8 + s, n, stride=s8), :] for s in range(s8)], axis=1)


def _token_copy(src_ref, dst_ref, sem, src_tok, dst_tok, s8):
    return pltpu.make_async_copy(
        src_ref.at[pl.ds(pl.multiple_of(src_tok * s8, s8), s8), :],
        dst_ref.at[pl.ds(pl.multiple_of(dst_tok * s8, s8), s8), :], sem)


def _check_token_tiles(d):
    assert d % (SUBLANES * LANES) == 0, "token-tile layout needs whole (8, 128) tiles per token"
    return d // LANES


def _router_kernel(x_ref, g_ref, wr_ref, br_ref, hn_ref, idx_ref, gate_ref):
    hn = _rms(x_ref[...], g_ref[...])
    _store_token_tiles(hn_ref, hn)
    hn_hi, hn_lo = _split_bf16(hn)
    w_hi, w_lo = _split_bf16(wr_ref[...])
    logits = _dot(jnp.concatenate([hn_hi, hn_lo, hn_hi], axis=1),
                  jnp.concatenate([w_hi, w_hi, w_lo], axis=0)) + br_ref[...]
    n_e = logits.shape[1]
    lane = lax.broadcasted_iota(I32, logits.shape, 1)
    v1 = jnp.max(logits, axis=-1, keepdims=True)
    i1 = jnp.min(jnp.where(logits == v1, lane, n_e), axis=-1, keepdims=True)
    rest = jnp.where(lane == i1, -jnp.inf, logits)
    v2 = jnp.max(rest, axis=-1, keepdims=True)
    i2 = jnp.min(jnp.where(rest == v2, lane, n_e), axis=-1, keepdims=True)
    e2 = jnp.exp(v2 - v1)
    first = lax.broadcasted_iota(I32, idx_ref.shape, 1) == 0
    idx_ref[...] = jnp.where(first, i1, i2)
    gate_ref[...] = jnp.where(first, 1.0, e2) / (1.0 + e2)


def router(h, g, wr, br, *, tm=1024):
    M, D = h.shape
    E = wr.shape[1]
    s8 = _check_token_tiles(D)
    tm = _tile(M, tm)
    return pl.pallas_call(
        _router_kernel,
        grid=(M // tm,),
        in_specs=[
            pl.BlockSpec((tm, D), lambda i: (i, 0)),
            pl.BlockSpec((1, D), lambda i: (0, 0)),
            pl.BlockSpec((D, E), lambda i: (0, 0)),
            pl.BlockSpec((1, E), lambda i: (0, 0)),
        ],
        out_specs=[
            pl.BlockSpec((tm * s8, LANES), lambda i: (i, 0)),
            pl.BlockSpec((tm, TOP_K), lambda i: (i, 0)),
            pl.BlockSpec((tm, TOP_K), lambda i: (i, 0)),
        ],
        out_shape=[
            jax.ShapeDtypeStruct((M * s8, LANES), F32),
            jax.ShapeDtypeStruct((M, TOP_K), I32),
            jax.ShapeDtypeStruct((M, TOP_K), F32),
        ],
        compiler_params=_params("parallel"),
        name="router",
    )(h, g.reshape(1, D), wr, br.reshape(1, E))


def _dispatch_kernel(dest_ref, hn_ref, xs_in_hbm, xs_hbm, sem, *, s8):
    del xs_in_hbm
    tq = hn_ref.shape[0] // s8

    def issue(t, carry):
        for k in range(TOP_K):
            _token_copy(hn_ref, xs_hbm, sem, t, dest_ref[TOP_K * t + k], s8).start()
        return carry

    lax.fori_loop(0, tq, issue, 0)
    for _ in range(TOP_K):
        pltpu.make_async_copy(hn_ref, xs_hbm.at[pl.ds(0, tq * s8), :], sem).wait()


def dispatch_rows(hn, dest, n_rows, *, s8, init=None, tq=512):
    if init is None:
        init = jnp.zeros((n_rows * s8, LANES), hn.dtype)
    T = hn.shape[0] // s8
    tq = _tile(T, tq)
    return pl.pallas_call(
        functools.partial(_dispatch_kernel, s8=s8),
        grid=(T // tq,),
        in_specs=[
            pl.BlockSpec((TOP_K * tq,), lambda i: (i,), memory_space=pltpu.SMEM),
            pl.BlockSpec((tq * s8, LANES), lambda i: (i, 0)),
            pl.BlockSpec(memory_space=pl.ANY),
        ],
        out_specs=pl.BlockSpec(memory_space=pl.ANY),
        out_shape=jax.ShapeDtypeStruct((n_rows * s8, LANES), hn.dtype),
        scratch_shapes=[pltpu.SemaphoreType.DMA],
        input_output_aliases={2: 0},
        compiler_params=_params("arbitrary"),
        name="dispatch_rows",
    )(dest, hn, init)


def _expert_kernel(be_ref, nu_ref, x_ref, wg_ref, wu_ref, wd_ref, o_ref):
    used = pl.program_id(0) < nu_ref[0]
    d = wg_ref.shape[0]
    s8 = d // LANES
    tm = x_ref.shape[0] // s8

    @pl.when(used)
    def _():
        x = _load_token_tiles(x_ref, 0, tm, s8).astype(BF16)
        a = (jax.nn.silu(_dot(x, wg_ref[...])) * _dot(x, wu_ref[...])).astype(BF16)
        _store_token_tiles(o_ref, _dot(a, wd_ref[...]))

    @pl.when(jnp.logical_not(used))
    def _():
        o_ref[...] = jnp.zeros_like(o_ref)


def expert_ffn(xs, block_expert, n_used, wg, wu, wd, *, layer, tm):
    D, F = wg.shape[2], wg.shape[3]
    s8 = D // LANES
    R = xs.shape[0] // s8
    return pl.pallas_call(
        _expert_kernel,
        grid_spec=pltpu.PrefetchScalarGridSpec(
            num_scalar_prefetch=2,
            grid=(R // tm,),
            in_specs=[
                pl.BlockSpec((tm * s8, LANES), lambda i, be, nu: (i, 0)),
                pl.BlockSpec((None, None, D, F), lambda i, be, nu: (layer, be[i], 0, 0)),
                pl.BlockSpec((None, None, D, F), lambda i, be, nu: (layer, be[i], 0, 0)),
                pl.BlockSpec((None, None, F, D), lambda i, be, nu: (layer, be[i], 0, 0)),
            ],
            out_specs=pl.BlockSpec((tm * s8, LANES), lambda i, be, nu: (i, 0)),
        ),
        out_shape=jax.ShapeDtypeStruct((R * s8, LANES), F32),
        compiler_params=_params("parallel"),
        name="expert_ffn",
    )(block_expert, n_used, xs, wg, wu, wd)


def _combine_kernel(dest_ref, dest_next_ref, ys_hbm, h_ref, gate_ref, g_ref, o_ref, buf_ref, sem, *, final_norm):
    tq, d = h_ref.shape
    s8 = d // LANES
    i = pl.program_id(0)
    cur = i % 2

    def gather(d_ref, slot):
        def issue(t, carry):
            for k in range(TOP_K):
                _token_copy(ys_hbm, buf_ref.at[slot], sem.at[slot], d_ref[TOP_K * t + k], k * tq + t, s8).start()
            return carry

        lax.fori_loop(0, tq, issue, 0)

    @pl.when(i == 0)
    def _():
        gather(dest_ref, 0)

    @pl.when(i + 1 < pl.num_programs(0))
    def _():
        gather(dest_next_ref, 1 - cur)

    pltpu.make_async_copy(ys_hbm.at[pl.ds(0, TOP_K * tq * s8), :], buf_ref.at[cur], sem.at[cur]).wait()

    def rows(k):
        return jnp.concatenate(
            [buf_ref[cur, pl.ds(k * tq * s8 + s, tq, stride=s8), :] for s in range(s8)], axis=1)

    gate = gate_ref[...]
    ff = rows(0) * gate[:, 0:1]
    for k in range(1, TOP_K):
        ff = ff + rows(k) * gate[:, k:k + 1]
    out = h_ref[...] + ff
    if final_norm:
        out = _rms(out, g_ref[...])
    o_ref[...] = out


def combine(h, ys, dest, gates, final_gain, *, tq=512):
    T, D = h.shape
    s8 = D // LANES
    tq = _tile(T, tq)
    final_norm = final_gain is not None
    gain = final_gain if final_norm else jnp.ones((D,), F32)
    n_tiles = T // tq
    return pl.pallas_call(
        functools.partial(_combine_kernel, final_norm=final_norm),
        grid=(n_tiles,),
        in_specs=[
            pl.BlockSpec((TOP_K * tq,), lambda i: (i,), memory_space=pltpu.SMEM),
            pl.BlockSpec((TOP_K * tq,), lambda i: (jnp.minimum(i + 1, n_tiles - 1),), memory_space=pltpu.SMEM),
            pl.BlockSpec(memory_space=pl.ANY),
            pl.BlockSpec((tq, D), lambda i: (i, 0)),
            pl.BlockSpec((tq, TOP_K), lambda i: (i, 0)),
            pl.BlockSpec((1, D), lambda i: (0, 0)),
        ],
        out_specs=pl.BlockSpec((tq, D), lambda i: (i, 0)),
        out_shape=jax.ShapeDtypeStruct((T, D), F32),
        scratch_shapes=[pltpu.VMEM((2, TOP_K * tq * s8, LANES), F32), pltpu.SemaphoreType.DMA((2,))],
        compiler_params=_params("arbitrary"),
        name="combine",
    )(dest, dest, ys, h, gates, gain.reshape(1, D))


def _final_norm_kernel(x_ref, g_ref, o_ref):
    o_ref[...] = _rms(x_ref[...], g_ref[...])


def final_norm(h, g, *, tm=1024):
    M, D = h.shape
    tm = _tile(M, tm)
    return pl.pallas_call(
        _final_norm_kernel,
        grid=(M // tm,),
        in_specs=[pl.BlockSpec((tm, D), lambda i: (i, 0)), pl.BlockSpec((1, D), lambda i: (0, 0))],
        out_specs=pl.BlockSpec((tm, D), lambda i: (i, 0)),
        out_shape=jax.ShapeDtypeStruct((M, D), F32),
        compiler_params=_params("parallel"),
        name="final_norm",
    )(h, g.reshape(1, D))


def _dft_cos_sin(n):
    j = lax.broadcasted_iota(I32, (n, n), 0)
    k = lax.broadcasted_iota(I32, (n, n), 1)
    ang = ((j * k) % n).astype(F32) * (2.0 * np.pi / n)
    return jnp.cos(ang), jnp.sin(ang)


def _route(idx, n_experts, tm):
    T = idx.shape[0]
    TK = T * TOP_K
    onehot = (idx[:, :, None] == jnp.arange(n_experts, dtype=I32)[None, None, :]).astype(I32)
    per_tok = jnp.sum(onehot, axis=1)
    before = jnp.cumsum(per_tok, axis=0) - per_tok
    counts = before[-1] + per_tok[-1]
    padded = (counts + tm - 1) // tm * tm
    pend = jnp.cumsum(padded)
    pstart = pend - padded
    dest = jnp.sum(onehot * (before + pstart[None, :])[:, None, :], axis=2).reshape(TK).astype(I32)
    n_blocks = -(-(TK + n_experts * (tm - 1)) // tm)
    block_expert = jnp.minimum(
        jnp.searchsorted(pend, jnp.arange(n_blocks, dtype=I32) * tm, side="right"), n_experts - 1).astype(I32)
    n_used = (pend[-1:] // tm).astype(I32)
    return dest, block_expert, n_used, n_blocks * tm


def _moe_block_rows(T):
    return _tile(T * TOP_K, 256)


def kernel(x, norm_mix, norm_ffn, fourier_w_in, fourier_w_out, nat_w_qkv, nat_rpb, nat_w_o,
           ffn_w_gate, ffn_w_up, ffn_w_down, router_w, router_b,
           expert_w_gate, expert_w_up, expert_w_down, final_norm_g):
    B, S, D = x.shape
    T = B * S
    depth = norm_mix.shape[0]
    n_experts = router_w.shape[2]
    cg = D // N_FOURIER_GROUPS

    cc, sc = _dft_cos_sin(cg)
    eye = jnp.eye(N_FOURIER_GROUPS, dtype=F32)
    chan = jnp.concatenate([jnp.kron(eye, cc), jnp.kron(eye, sc)], axis=1)
    dft_scale = float(1.0 / np.sqrt(S * cg))
    r1, r2 = S // GRID_W, GRID_W
    ewg, ewu, ewd = (w.astype(BF16) for w in (expert_w_gate, expert_w_up, expert_w_down))

    h = x.reshape(T, D)
    xs = None
    for i in range(depth):
        j = i // 2
        last = i == depth - 1
        if i % 2 == 0:
            w_ab = matmul_f32(fourier_w_in[j], chan, out_dtype=BF16)
            q = fourier_in(h, norm_mix[i], w_ab, batch=B, r1=r1, r2=r2)
            h = fft_stage2_out(q, fourier_w_out[j].astype(BF16), h, batch=B, r1=r1, r2=r2, scale=dft_scale)
            h = ffn(h, norm_ffn[i], ffn_w_gate[j].astype(BF16), ffn_w_up[j].astype(BF16),
                    ffn_w_down[j].astype(BF16))
            if last:
                h = final_norm(h, final_norm_g)
        else:
            kw = (nat_rpb.shape[3] + 1) // 2
            q, k, v = qkv_proj(h, norm_mix[i], nat_w_qkv[j].astype(BF16), kw=kw)
            att = attention(q, k, v, nat_rpb[j], batch=B)
            h = matmul_res(att, nat_w_o[j].astype(BF16), h)
            hn, idx, gates = router(h, norm_ffn[i], router_w[j], router_b[j])
            tm = _moe_block_rows(T)
            dest, block_expert, n_used, n_rows = _route(idx, n_experts, tm)
            xs = dispatch_rows(hn, dest, n_rows, s8=D // LANES, init=xs)
            ys = expert_ffn(xs, block_expert, n_used, ewg, ewu, ewd, layer=j, tm=tm)
            h = combine(h, ys, dest, gates, final_norm_g if last else None)
    return h.reshape(B, S, D)
```

```python
import functools

import numpy as np
import jax
import jax.numpy as jnp
from jax import lax
from jax.experimental import pallas as pl
from jax.experimental.pallas import tpu as pltpu

F32 = jnp.float32
BF16 = jnp.bfloat16
I32 = jnp.int32

GRID_W = 64
N_FOURIER_GROUPS = 4
TOP_K = 2
RMS_EPS = 1e-6
MASK_VALUE = -1e30

LANES = 128
SUBLANES = 8

V7X_VMEM_BYTES = 64 * 1024 * 1024
VMEM_LIMIT_BYTES = V7X_VMEM_BYTES - 8 * 1024 * 1024


def _tile(n, target):
    if n <= target:
        return n
    for t in range(target, 0, -1):
        if n % t == 0:
            return t
    return n


def _params(*sem):
    return pltpu.CompilerParams(dimension_semantics=sem, vmem_limit_bytes=VMEM_LIMIT_BYTES)


def _rms(x, g):
    ms = jnp.mean(x * x, axis=-1, keepdims=True)
    return x * lax.rsqrt(ms + RMS_EPS) * g


def _dot(a, b):
    return jnp.dot(a, b, preferred_element_type=F32)


def _attn_col_plan(kw):
    W = GRID_W
    n_cb = W // kw
    wk = min(W, 2 * kw)
    starts = np.clip(np.arange(n_cb) * kw - kw // 2, 0, W - wk)
    c = np.arange(W)
    col_start = np.clip(c - kw // 2, 0, W - kw)
    blk = c // kw
    assert (col_start >= starts[blk]).all() and (col_start + kw <= starts[blk] + wk).all()
    return n_cb, wk, [int(s) for s in starts]


def _qkv_kernel(x_ref, g_ref, w_ref, q_ref, k_ref, v_ref, *, col_starts, wk):
    tm, d = q_ref.shape
    y = _dot(_rms(x_ref[...], g_ref[...]).astype(BF16), w_ref[...])
    q_ref[...] = y[:, :d].astype(BF16)
    n_cb = len(col_starts)
    for r in range(tm // GRID_W):
        for j, cs in enumerate(col_starts):
            src = slice(r * GRID_W + cs, r * GRID_W + cs + wk)
            dst = slice((r * n_cb + j) * wk, (r * n_cb + j + 1) * wk)
            k_ref[dst, :] = y[src, d:2 * d].astype(BF16)
            v_ref[dst, :] = y[src, 2 * d:].astype(BF16)


def qkv_proj(h, g, w, *, kw, tm=512):
    M, K = h.shape
    D = w.shape[1] // 3
    n_cb, wk, col_starts = _attn_col_plan(kw)
    tm = _tile(M, tm)
    assert tm % GRID_W == 0
    tkv = tm // GRID_W * n_cb * wk
    kv_rows = M // GRID_W * n_cb * wk
    return pl.pallas_call(
        functools.partial(_qkv_kernel, col_starts=col_starts, wk=wk),
        grid=(M // tm,),
        in_specs=[
            pl.BlockSpec((tm, K), lambda i: (i, 0)),
            pl.BlockSpec((1, K), lambda i: (0, 0)),
            pl.BlockSpec((K, 3 * D), lambda i: (0, 0)),
        ],
        out_specs=[
            pl.BlockSpec((tm, D), lambda i: (i, 0)),
            pl.BlockSpec((tkv, D), lambda i: (i, 0)),
            pl.BlockSpec((tkv, D), lambda i: (i, 0)),
        ],
        out_shape=[
            jax.ShapeDtypeStruct((M, D), BF16),
            jax.ShapeDtypeStruct((kv_rows, D), BF16),
            jax.ShapeDtypeStruct((kv_rows, D), BF16),
        ],
        compiler_params=_params("parallel"),
        name="qkv_proj",
    )(h, g.reshape(1, K), w)


def _matmul_res_kernel(a_ref, w_ref, r_ref, o_ref):
    o_ref[...] = r_ref[...] + _dot(a_ref[...], w_ref[...])


def matmul_res(a, w, res, *, tm=1024, tn=1024):
    M, K = a.shape
    N = w.shape[1]
    tm, tn = _tile(M, tm), _tile(N, tn)
    return pl.pallas_call(
        _matmul_res_kernel,
        grid=(M // tm, N // tn),
        in_specs=[
            pl.BlockSpec((tm, K), lambda i, j: (i, 0)),
            pl.BlockSpec((K, tn), lambda i, j: (0, j)),
            pl.BlockSpec((tm, tn), lambda i, j: (i, j)),
        ],
        out_specs=pl.BlockSpec((tm, tn), lambda i, j: (i, j)),
        out_shape=jax.ShapeDtypeStruct((M, N), F32),
        compiler_params=_params("parallel", "parallel"),
        name="matmul_res",
    )(a, w, res)


def _matmul_f32_kernel(a_ref, b_ref, o_ref):
    o_ref[...] = jnp.dot(a_ref[...], b_ref[...], preferred_element_type=F32,
                         precision=lax.Precision.HIGHEST).astype(o_ref.dtype)


def matmul_f32(a, b, *, out_dtype, tm=512, tn=512):
    M, K = a.shape
    N = b.shape[1]
    tm, tn = _tile(M, tm), _tile(N, tn)
    return pl.pallas_call(
        _matmul_f32_kernel,
        grid=(M // tm, N // tn),
        in_specs=[
            pl.BlockSpec((tm, K), lambda i, j: (i, 0)),
            pl.BlockSpec((K, tn), lambda i, j: (0, j)),
        ],
        out_specs=pl.BlockSpec((tm, tn), lambda i, j: (i, j)),
        out_shape=jax.ShapeDtypeStruct((M, N), out_dtype),
        compiler_params=_params("parallel", "parallel"),
        name="matmul_f32",
    )(a, b)


def _dft_np(n):
    j = np.arange(n)
    ang = 2.0 * np.pi * ((j[:, None] * j[None, :]) % n) / n
    return np.cos(ang), np.sin(ang)


def _fourier_in_kernel(x_ref, g_ref, w_ref, ga_ref, gb_ref, twc_ref, tws_ref, o_ref):
    r1, n_c, dm = x_ref.shape
    n = r1 * n_c
    xn = _rms(x_ref[...].reshape(n, dm), g_ref[...]).astype(BF16)
    ab = _dot(xn, w_ref[...])
    d = ab.shape[1] // 2
    a, b = ab[:, :d].astype(BF16), ab[:, d:].astype(BF16)
    p = _dot(ga_ref[...], a) + _dot(gb_ref[...], b)
    pr, pi = p[:n], p[n:]
    tc, ts = twc_ref[...], tws_ref[...]
    q = jnp.concatenate([pr * tc + pi * ts, pi * tc - pr * ts], axis=1)
    o_ref[...] = q.reshape(r1, n_c, 2 * d).astype(o_ref.dtype)


def fourier_in(h, g, w_ab, *, batch, r1, r2, cols_per_step=SUBLANES):
    T, D = h.shape
    D2 = w_ab.shape[1]
    n_c = _tile(r2, cols_per_step)
    n = r1 * n_c
    c1, s1 = _dft_np(r1)
    eye = np.eye(n_c)
    ga = jnp.asarray(np.kron(np.concatenate([c1, -s1], axis=0), eye), BF16)
    gb = jnp.asarray(np.kron(np.concatenate([-s1, -c1], axis=0), eye), BF16)
    c = np.arange(r2).reshape(r2 // n_c, 1, n_c)
    kc = np.arange(r1).reshape(1, r1, 1)
    ang = (2.0 * np.pi * (c * kc) / (r1 * r2)).reshape(r2 // n_c, n, 1)
    twc, tws = jnp.asarray(np.cos(ang), F32), jnp.asarray(np.sin(ang), F32)
    out = pl.pallas_call(
        _fourier_in_kernel,
        grid=(batch, r2 // n_c),
        in_specs=[
            pl.BlockSpec((None, r1, n_c, D), lambda b, c: (b, 0, c, 0)),
            pl.BlockSpec((1, D), lambda b, c: (0, 0)),
            pl.BlockSpec((D, D2), lambda b, c: (0, 0)),
            pl.BlockSpec((2 * n, n), lambda b, c: (0, 0)),
            pl.BlockSpec((2 * n, n), lambda b, c: (0, 0)),
            pl.BlockSpec((None, n, 1), lambda b, c: (c, 0, 0)),
            pl.BlockSpec((None, n, 1), lambda b, c: (c, 0, 0)),
        ],
        out_specs=pl.BlockSpec((None, r1, n_c, D2), lambda b, c: (b, 0, c, 0)),
        out_shape=jax.ShapeDtypeStruct((batch, r1, r2, D2), BF16),
        compiler_params=_params("parallel", "parallel"),
        name="fourier_in",
    )(h.reshape(batch, r1, r2, D), g.reshape(1, D), w_ab, ga, gb, twc, tws)
    return out.reshape(T, D2)


def _fft_stage2_kernel(c_ref, s_ref, q_ref, w_ref, r_ref, o_ref, *, scale):
    kb, r2, d2 = q_ref.shape
    d = d2 // 2
    fs = []
    for j in range(kb):
        qr, qi = q_ref[j, :, :d].astype(BF16), q_ref[j, :, d:].astype(BF16)
        f = _dot(c_ref[...], qr) + _dot(s_ref[...], qi)
        fs.append((f * scale).astype(BF16))
    g = _dot(jnp.concatenate(fs, axis=0), w_ref[...])
    for j in range(kb):
        o_ref[:, j, :] = r_ref[:, j, :] + g[j * r2:(j + 1) * r2, :]


def fft_stage2_out(q, w_out, res, *, batch, r1, r2, scale, kc_per_step=8):
    T, D2 = q.shape
    D = D2 // 2
    kb = _tile(r1, kc_per_step)
    c2, s2 = _dft_np(r2)
    q4 = q.reshape(batch, r1, r2, D2)
    res4 = res.reshape(batch, r2, r1, D)
    out = pl.pallas_call(
        functools.partial(_fft_stage2_kernel, scale=scale),
        grid=(batch, r1 // kb),
        in_specs=[
            pl.BlockSpec((r2, r2), lambda b, k: (0, 0)),
            pl.BlockSpec((r2, r2), lambda b, k: (0, 0)),
            pl.BlockSpec((None, kb, r2, D2), lambda b, k: (b, k, 0, 0)),
            pl.BlockSpec((D, D), lambda b, k: (0, 0)),
            pl.BlockSpec((None, r2, kb, D), lambda b, k: (b, 0, k, 0)),
        ],
        out_specs=pl.BlockSpec((None, r2, kb, D), lambda b, k: (b, 0, k, 0)),
        out_shape=jax.ShapeDtypeStruct((batch, r2, r1, D), F32),
        compiler_params=_params("parallel", "parallel"),
        name="fft_stage2_out",
    )(jnp.asarray(c2, BF16), jnp.asarray(s2, BF16), q4, w_out, res4)
    return out.reshape(T, D)


def _ffn_kernel(x_ref, g_ref, wg_ref, wu_ref, wd_ref, o_ref):
    x = x_ref[...]
    xn = _rms(x, g_ref[...]).astype(BF16)
    a = (jax.nn.silu(_dot(xn, wg_ref[...])) * _dot(xn, wu_ref[...])).astype(BF16)
    o_ref[...] = x + _dot(a, wd_ref[...])


def ffn(h, g, wg, wu, wd, *, tm=512):
    M, D = h.shape
    F = wg.shape[1]
    tm = _tile(M, tm)
    return pl.pallas_call(
        _ffn_kernel,
        grid=(M // tm,),
        in_specs=[
            pl.BlockSpec((tm, D), lambda i: (i, 0)),
            pl.BlockSpec((1, D), lambda i: (0, 0)),
            pl.BlockSpec((D, F), lambda i: (0, 0)),
            pl.BlockSpec((D, F), lambda i: (0, 0)),
            pl.BlockSpec((F, D), lambda i: (0, 0)),
        ],
        out_specs=pl.BlockSpec((tm, D), lambda i: (i, 0)),
        out_shape=jax.ShapeDtypeStruct((M, D), F32),
        compiler_params=_params("parallel"),
        name="ffn",
    )(h, g.reshape(1, D), wg, wu, wd)


def _attn_kernel(ks_ref, tid_ref, dr_ref, q_ref, k_ref, v_ref, tab_ref, o_ref, b_ref, *, n_heads, scale):
    NB, R, C, D = q_ref.shape
    KR, WK = k_ref.shape[1], k_ref.shape[3]
    hd = D // n_heads
    nq = R * C
    nk = KR * WK

    i, bidx = pl.program_id(1), pl.program_id(2)
    tid = tid_ref[i]
    changed = (i == 0) | (tid != tid_ref[jnp.maximum(i - 1, 0)])

    @pl.when((bidx == 0) & changed)
    def _():
        def per_head(h, carry):
            for rq in range(R):
                for rk in range(KR):
                    dr = dr_ref[(tid * R + rq) * KR + rk]
                    b_ref[h, rq * C:(rq + 1) * C, rk * WK:(rk + 1) * WK] = tab_ref[h, dr]
            return carry

        lax.fori_loop(0, n_heads, per_head, 0)

    lanes = min(D, max(hd, LANES))
    per_group = lanes // hd
    fold_scale = float(np.frexp(scale)[0]) == 0.5
    lane = lax.broadcasted_iota(I32, (1, lanes), 1)
    n_groups = D // lanes
    masks = [(lane >= hh * hd) & (lane < (hh + 1) * hd) for hh in range(per_group)]

    logits = []
    for bb in range(NB):
        for g in range(n_groups):
            sl = slice(g * lanes, (g + 1) * lanes)
            q = q_ref[bb, :, :, sl].reshape(nq, lanes)
            k = k_ref[bb, :, 0, :, sl].reshape(nk, lanes)
            if fold_scale:
                q = q * scale
            for hh in range(per_group):
                qh = jnp.where(masks[hh], q, jnp.zeros_like(q)) if per_group > 1 else q
                s = lax.dot_general(qh, k, (((1,), (1,)), ((), ())), preferred_element_type=F32)
                if not fold_scale:
                    s = s * scale
                logits.append(s + b_ref[g * per_group + hh])
    probs, inv_ls = [], []
    for s in logits:
        p = jnp.exp(s - jnp.max(s, axis=-1, keepdims=True))
        inv_ls.append(1.0 / jnp.sum(p, axis=-1, keepdims=True))
        probs.append(p.astype(BF16))
    for bb in range(NB):
        for g in range(n_groups):
            sl = slice(g * lanes, (g + 1) * lanes)
            v = v_ref[bb, :, 0, :, sl].reshape(nk, lanes)
            out = None
            for hh in range(per_group):
                h = (bb * n_groups + g) * per_group + hh
                o = _dot(probs[h], v) * inv_ls[h]
                out = o if out is None else jnp.where(masks[hh], o, out)
            o_ref[bb, :, :, sl] = out.reshape(R, C, lanes).astype(o_ref.dtype)


def _attn_plan(rows, kh, R):
    KR = min(rows, R + kh)
    r = np.arange(rows)
    row_start = np.clip(r - kh // 2, 0, rows - kh)
    nblk = rows // R
    ks = np.clip(np.arange(nblk) * R - kh // 2, 0, rows - KR)
    sigs, tid = [], []
    for i in range(nblk):
        rs = row_start[i * R:(i + 1) * R]
        assert (rs >= ks[i]).all() and (rs + kh <= ks[i] + KR).all()
        sig = (i * R - ks[i],) + tuple(rs - ks[i])
        if sig not in sigs:
            sigs.append(sig)
        tid.append(sigs.index(sig))
    return KR, ks.astype(np.int32), np.asarray(tid, np.int32), sigs


def _attn_bias_table(rpb, kh, kw, R, KR, sigs, wk, col_starts):
    H = rpb.shape[0]
    win_h_max = (rpb.shape[1] + 1) // 2
    W = GRID_W
    c = np.arange(W)
    col_start = np.clip(c - kw // 2, 0, W - kw)
    kc = np.arange(W)
    col_ok = (kc[None, :] >= col_start[:, None]) & (kc[None, :] < col_start[:, None] + kw)
    dc = kc[None, :] - c[:, None] + (kw - 1)
    onehot = (dc[None] == np.arange(2 * kw - 1)[:, None, None]) & col_ok[None]
    tab = jnp.einsum("hrd,dck->hrck", rpb.astype(F32), jnp.asarray(onehot, F32),
                     precision=lax.Precision.HIGHEST)
    tab = jnp.where(jnp.asarray(col_ok), tab, MASK_VALUE)
    n_dr = 2 * win_h_max - 1
    tab = jnp.concatenate([tab, jnp.full((H, 1, W, W), MASK_VALUE, F32)], axis=1)
    tab = jnp.stack([tab[:, :, j * kw:(j + 1) * kw, cs:cs + wk] for j, cs in enumerate(col_starts)], axis=0)
    dr_of =np.full((len(sigs), R, KR), n_dr, np.int32)
    for t, sig in enumerate(sigs):
        q0, rs = sig[0], sig[1:]
        for rq in range(R):
            for rk in range(KR):
                if rs[rq] <= rk < rs[rq] + kh:
                    dr_of[t, rq, rk] = rk - (q0 + rq) + (win_h_max - 1)
    assert dr_of.min() >= 0
    return tab, dr_of.reshape(-1)


def attention(q, k, v, rpb, *, batch, rows_per_step=8, batch_per_step=4):
    T, D = q.shape
    nb = _tile(batch, batch_per_step)
    S = T // batch
    W = GRID_W
    rows = S // W
    H = rpb.shape[0]
    kh = min((rpb.shape[1] + 1) // 2, rows)
    kw = (rpb.shape[2] + 1) // 2
    hd = D // H
    n_cb, wk, col_starts = _attn_col_plan(kw)
    R = _tile(rows, rows_per_step)
    KR, ks, tid, sigs = _attn_plan(rows, kh, R)
    tab, dr_of = _attn_bias_table(rpb, kh, kw, R, KR, sigs, wk, col_starts)
    q5 = q.reshape(batch, rows, n_cb, kw, D)
    k5 = k.reshape(batch, rows, n_cb, wk, D)
    v5 = v.reshape(batch, rows, n_cb, wk, D)

    kv_spec = pl.BlockSpec(
        (pl.Element(nb), pl.Element(KR), pl.Element(1), pl.Element(wk), pl.Element(D)),
        lambda j, i, b, ks_ref, tid_ref, dr_ref: (b * nb, ks_ref[i], j, 0, 0))
    q_spec = pl.BlockSpec((nb, R, None, kw, D), lambda j, i, b, ks_ref, tid_ref, dr_ref: (b, i, j, 0, 0))

    out = pl.pallas_call(
        functools.partial(_attn_kernel, n_heads=H, scale=hd ** -0.5),
        grid_spec=pltpu.PrefetchScalarGridSpec(
            num_scalar_prefetch=3,
            grid=(n_cb, rows // R, batch // nb),
            in_specs=[
                q_spec,
                kv_spec,
                kv_spec,
                pl.BlockSpec((None,) + tab.shape[1:], lambda j, i, b, ks_ref, tid_ref, dr_ref: (j, 0, 0, 0, 0)),
            ],
            out_specs=q_spec,
            scratch_shapes=[pltpu.VMEM((H, R * kw, KR * wk), F32)],
        ),
        out_shape=jax.ShapeDtypeStruct((batch, rows, n_cb, kw, D), BF16),
        compiler_params=_params("arbitrary", "arbitrary", "arbitrary"),
        name="attention",
    )(jnp.asarray(ks), jnp.asarray(tid), jnp.asarray(dr_of), q5, k5, v5, tab)
    return out.reshape(T, D)


def _store_token_tiles(dst_ref, x):
    n, d = x.shape
    s8 = d // LANES
    for s in range(s8):
        dst_ref[pl.ds(s, n, stride=s8), :] = x[:, s * LANES:(s + 1) * LANES]


def _load_token_tiles(src_ref, first, n, s8):
    return jnp.concatenate(
        [src_ref[pl.ds(first * s8 + s, n, stride=s8), :] for s in range(s8)], axis=1)


def _token_copy(src_ref, dst_ref, sem, src_tok, dst_tok, s8):
    return pltpu.make_async_copy(
        src_ref.at[pl.ds(pl.multiple_of(src_tok * s8, s8), s8), :],
        dst_ref.at[pl.ds(pl.multiple_of(dst_tok * s8, s8), s8), :], sem)


def _check_token_tiles(d):
    assert d % (SUBLANES * LANES) == 0, "token-tile layout needs whole (8, 128) tiles per token"
    return d // LANES


def _router_kernel(x_ref, g_ref, wr_ref, br_ref, hn_ref, idx_ref, gate_ref):
    hn = _rms(x_ref[...], g_ref[...])
    _store_token_tiles(hn_ref, hn)
    w = wr_ref[...]
    hn_hi, w_hi = hn.astype(BF16), w.astype(BF16)
    hn_lo = (hn - hn_hi.astype(F32)).astype(BF16)
    w_lo = (w - w_hi.astype(F32)).astype(BF16)
    logits = _dot(jnp.concatenate([hn_hi, hn_lo, hn_hi], axis=1),
                  jnp.concatenate([w_hi, w_hi, w_lo], axis=0)) + br_ref[...]
    n_e = logits.shape[1]
    lane = lax.broadcasted_iota(I32, logits.shape, 1)
    v1 = jnp.max(logits, axis=-1, keepdims=True)
    i1 = jnp.min(jnp.where(logits == v1, lane, n_e), axis=-1, keepdims=True)
    rest = jnp.where(lane == i1, -jnp.inf, logits)
    v2 = jnp.max(rest, axis=-1, keepdims=True)
    i2 = jnp.min(jnp.where(rest == v2, lane, n_e), axis=-1, keepdims=True)
    e2 = jnp.exp(v2 - v1)
    first = lax.broadcasted_iota(I32, idx_ref.shape, 1) == 0
    idx_ref[...] = jnp.where(first, i1, i2)
    gate_ref[...] = jnp.where(first, 1.0, e2) / (1.0 + e2)


def router(h, g, wr, br, *, tm=1024):
    M, D = h.shape
    E = wr.shape[1]
    s8 = _check_token_tiles(D)
    tm = _tile(M, tm)
    return pl.pallas_call(
        _router_kernel,
        grid=(M // tm,),
        in_specs=[
            pl.BlockSpec((tm, D), lambda i: (i, 0)),
            pl.BlockSpec((1, D), lambda i: (0, 0)),
            pl.BlockSpec((D, E), lambda i: (0, 0)),
            pl.BlockSpec((1, E), lambda i: (0, 0)),
        ],
        out_specs=[
            pl.BlockSpec((tm * s8, LANES), lambda i: (i, 0)),
            pl.BlockSpec((tm, TOP_K), lambda i: (i, 0)),
            pl.BlockSpec((tm, TOP_K), lambda i: (i, 0)),
        ],
        out_shape=[
            jax.ShapeDtypeStruct((M * s8, LANES), F32),
            jax.ShapeDtypeStruct((M, TOP_K), I32),
            jax.ShapeDtypeStruct((M, TOP_K), F32),
        ],
        compiler_params=_params("parallel"),
        name="router",
    )(h, g.reshape(1, D), wr, br.reshape(1, E))


def _dispatch_kernel(dest_ref, hn_ref, xs_in_hbm, xs_hbm, sem, *, s8):
    del xs_in_hbm
    tq = hn_ref.shape[0] // s8

    def issue(t, carry):
        for k in range(TOP_K):
            _token_copy(hn_ref, xs_hbm, sem, t, dest_ref[TOP_K * t + k], s8).start()
        return carry

    lax.fori_loop(0, tq, issue, 0)
    for _ in range(TOP_K):
        pltpu.make_async_copy(hn_ref, xs_hbm.at[pl.ds(0, tq * s8), :], sem).wait()


def dispatch_rows(hn, dest, n_rows, *, s8, init=None, tq=512):
    if init is None:
        init = jnp.zeros((n_rows * s8, LANES), hn.dtype)
    T = hn.shape[0] // s8
    tq = _tile(T, tq)
    return pl.pallas_call(
        functools.partial(_dispatch_kernel, s8=s8),
        grid=(T // tq,),
        in_specs=[
            pl.BlockSpec((TOP_K * tq,), lambda i: (i,), memory_space=pltpu.SMEM),
            pl.BlockSpec((tq * s8, LANES), lambda i: (i, 0)),
            pl.BlockSpec(memory_space=pl.ANY),
        ],
        out_specs=pl.BlockSpec(memory_space=pl.ANY),
        out_shape=jax.ShapeDtypeStruct((n_rows * s8, LANES), hn.dtype),
        scratch_shapes=[pltpu.SemaphoreType.DMA],
        input_output_aliases={2: 0},
        compiler_params=_params("arbitrary"),
        name="dispatch_rows",
    )(dest, hn, init)


def _expert_kernel(be_ref, nu_ref, x_ref, wg_ref, wu_ref, wd_ref, o_ref):
    used = pl.program_id(0) < nu_ref[0]
    d = wg_ref.shape[0]
    s8 = d // LANES
    tm = x_ref.shape[0] // s8

    @pl.when(used)
    def _():
        x = _load_token_tiles(x_ref, 0, tm, s8).astype(BF16)
        a = (jax.nn.silu(_dot(x, wg_ref[...])) * _dot(x, wu_ref[...])).astype(BF16)
        _store_token_tiles(o_ref, _dot(a, wd_ref[...]))

    @pl.when(jnp.logical_not(used))
    def _():
        o_ref[...] = jnp.zeros_like(o_ref)


def expert_ffn(xs, block_expert, n_used, wg, wu, wd, *, layer, tm):
    D, F = wg.shape[2], wg.shape[3]
    s8 = D // LANES
    R = xs.shape[0] // s8
    return pl.pallas_call(
        _expert_kernel,
        grid_spec=pltpu.PrefetchScalarGridSpec(
            num_scalar_prefetch=2,
            grid=(R // tm,),
            in_specs=[
                pl.BlockSpec((tm * s8, LANES), lambda i, be, nu: (i, 0)),
                pl.BlockSpec((None, None, D, F), lambda i, be, nu: (layer, be[i], 0, 0)),
                pl.BlockSpec((None, None, D, F), lambda i, be, nu: (layer, be[i], 0, 0)),
                pl.BlockSpec((None, None, F, D), lambda i, be, nu: (layer, be[i], 0, 0)),
            ],
            out_specs=pl.BlockSpec((tm * s8, LANES), lambda i, be, nu: (i, 0)),
        ),
        out_shape=jax.ShapeDtypeStruct((R * s8, LANES), F32),
        compiler_params=_params("parallel"),
        name="expert_ffn",
    )(block_expert, n_used, xs, wg, wu, wd)


def _combine_kernel(dest_ref, dest_next_ref, ys_hbm, h_ref, gate_ref, g_ref, o_ref, buf_ref, sem, *, final_norm):
    tq, d = h_ref.shape
    s8 = d // LANES
    i = pl.program_id(0)
    cur = i % 2

    def gather(d_ref, slot):
        def issue(t, carry):
            for k in range(TOP_K):
                _token_copy(ys_hbm, buf_ref.at[slot], sem.at[slot], d_ref[TOP_K * t + k], k * tq + t, s8).start()
            return carry

        lax.fori_loop(0, tq, issue, 0)

    @pl.when(i == 0)
    def _():
        gather(dest_ref, 0)

    @pl.when(i + 1 < pl.num_programs(0))
    def _():
        gather(dest_next_ref, 1 - cur)

    pltpu.make_async_copy(ys_hbm.at[pl.ds(0, TOP_K * tq * s8), :], buf_ref.at[cur], sem.at[cur]).wait()

    def rows(k):
        return jnp.concatenate(
            [buf_ref[cur, pl.ds(k * tq * s8 + s, tq, stride=s8), :] for s in range(s8)], axis=1)

    gate = gate_ref[...]
    ff = rows(0) * gate[:, 0:1]
    for k in range(1, TOP_K):
        ff = ff + rows(k) * gate[:, k:k + 1]
    out = h_ref[...] + ff
    if final_norm:
        out = _rms(out, g_ref[...])
    o_ref[...] = out


def combine(h, ys, dest, gates, final_gain, *, tq=512):
    T, D = h.shape
    s8 = D // LANES
    tq = _tile(T, tq)
    final_norm = final_gain is not None
    gain = final_gain if final_norm else jnp.ones((D,), F32)
    n_tiles = T // tq
    return pl.pallas_call(
        functools.partial(_combine_kernel, final_norm=final_norm),
        grid=(n_tiles,),
        in_specs=[
            pl.BlockSpec((TOP_K * tq,), lambda i: (i,), memory_space=pltpu.SMEM),
            pl.BlockSpec((TOP_K * tq,), lambda i: (jnp.minimum(i + 1, n_tiles - 1),), memory_space=pltpu.SMEM),
            pl.BlockSpec(memory_space=pl.ANY),
            pl.BlockSpec((tq, D), lambda i: (i, 0)),
            pl.BlockSpec((tq, TOP_K), lambda i: (i, 0)),
            pl.BlockSpec((1, D), lambda i: (0, 0)),
        ],
        out_specs=pl.BlockSpec((tq, D), lambda i: (i, 0)),
        out_shape=jax.ShapeDtypeStruct((T, D), F32),
        scratch_shapes=[pltpu.VMEM((2, TOP_K * tq * s8, LANES), F32), pltpu.SemaphoreType.DMA((2,))],
        compiler_params=_params("arbitrary"),
        name="combine",
    )(dest, dest, ys, h, gates, gain.reshape(1, D))


def _final_norm_kernel(x_ref, g_ref, o_ref):
    o_ref[...] = _rms(x_ref[...], g_ref[...])


def final_norm(h, g, *, tm=1024):
    M, D = h.shape
    tm = _tile(M, tm)
    return pl.pallas_call(
        _final_norm_kernel,
        grid=(M // tm,),
        in_specs=[pl.BlockSpec((tm, D), lambda i: (i, 0)), pl.BlockSpec((1, D), lambda i: (0, 0))],
        out_specs=pl.BlockSpec((tm, D), lambda i: (i, 0)),
        out_shape=jax.ShapeDtypeStruct((M, D), F32),
        compiler_params=_params("parallel"),
        name="final_norm",
    )(h, g.reshape(1, D))


def _dft_cos_sin(n):
    j = lax.broadcasted_iota(I32, (n, n), 0)
    k = lax.broadcasted_iota(I32, (n, n), 1)
    ang = ((j * k) % n).astype(F32) * (2.0 * np.pi / n)
    return jnp.cos(ang), jnp.sin(ang)


def _route(idx, n_experts, tm):
    T = idx.shape[0]
    TK = T * TOP_K
    flat_e = idx.reshape(TK)
    onehot = (flat_e[:, None] == jnp.arange(n_experts, dtype=I32)[None, :]).astype(I32)
    csum = jnp.cumsum(onehot, axis=0)
    rank = jnp.sum(csum * onehot, axis=1) - 1
    counts = csum[-1]
    padded = (counts + tm - 1) // tm * tm
    pend = jnp.cumsum(padded)
    pstart = pend - padded
    dest = (pstart[flat_e] + rank).astype(I32)
    n_blocks = -(-(TK + n_experts * (tm - 1)) // tm)
    block_expert = jnp.minimum(
        jnp.searchsorted(pend, jnp.arange(n_blocks, dtype=I32) * tm, side="right"), n_experts - 1).astype(I32)
    n_used = (pend[-1:] // tm).astype(I32)
    return dest, block_expert, n_used, n_blocks * tm


def _moe_block_rows(T):
    return _tile(T * TOP_K, 256)


def kernel(x, norm_mix, norm_ffn, fourier_w_in, fourier_w_out, nat_w_qkv, nat_rpb, nat_w_o,
           ffn_w_gate, ffn_w_up, ffn_w_down, router_w, router_b,
           expert_w_gate, expert_w_up, expert_w_down, final_norm_g):
    B, S, D = x.shape
    T = B * S
    depth = norm_mix.shape[0]
    n_experts = router_w.shape[2]
    cg = D // N_FOURIER_GROUPS

    cc, sc = _dft_cos_sin(cg)
    eye = jnp.eye(N_FOURIER_GROUPS, dtype=F32)
    chan = jnp.concatenate([jnp.kron(eye, cc), jnp.kron(eye, sc)], axis=1)
    dft_scale = float(1.0 / np.sqrt(S * cg))
    r1, r2 = S // GRID_W, GRID_W
    ewg, ewu, ewd = (w.astype(BF16) for w in (expert_w_gate, expert_w_up, expert_w_down))

    h = x.reshape(T, D)
    xs = None
    for i in range(depth):
        j = i // 2
        last = i == depth - 1
        if i % 2 == 0:
            w_ab = matmul_f32(fourier_w_in[j], chan, out_dtype=BF16)
            q = fourier_in(h, norm_mix[i], w_ab, batch=B, r1=r1, r2=r2)
            h = fft_stage2_out(q, fourier_w_out[j].astype(BF16), h, batch=B, r1=r1, r2=r2, scale=dft_scale)
            h = ffn(h, norm_ffn[i], ffn_w_gate[j].astype(BF16), ffn_w_up[j].astype(BF16),
                    ffn_w_down[j].astype(BF16))
            if last:
                h = final_norm(h, final_norm_g)
        else:
            kw = (nat_rpb.shape[3] + 1) // 2
            q, k, v = qkv_proj(h, norm_mix[i], nat_w_qkv[j].astype(BF16), kw=kw)
            att = attention(q, k, v, nat_rpb[j], batch=B)
            h = matmul_res(att, nat_w_o[j].astype(BF16), h)
            hn, idx, gates = router(h, norm_ffn[i], router_w[j], router_b[j])
            tm = _moe_block_rows(T)
            dest, block_expert, n_used, n_rows = _route(idx, n_experts, tm)
            xs = dispatch_rows(hn, dest, n_rows, s8=D // LANES, init=xs)
            ys = expert_ffn(xs, block_expert, n_used, ewg, ewu, ewd, layer=j, tm=tm)
            h = combine(h, ys, dest, gates, final_norm_g if last else None)
    return h.reshape(B, S, D)
```
